```python
import jax, jax.numpy as jnp
from jax import lax
import numpy as np

D_MODEL = 1024
BATCH = 4
SEQ = 4096
DEPTH = 1

GRID_W = 64
CTX_LEN = 256
D_CONV = 512
CONV_W = 3
D_RWKV = 512
HEAD_SIZE = 64
N_RWKV_HEADS = D_RWKV // HEAD_SIZE
D_MIX = D_CONV + D_RWKV
D_DECAY_LORA = 64
D_AAA_LORA = 64
D_GATE_LORA = 128
COL_SIZES = (D_CONV, D_CONV, D_CONV, D_RWKV, D_RWKV, D_RWKV, D_GATE_LORA, D_DECAY_LORA, D_DECAY_LORA, D_AAA_LORA, D_AAA_LORA)
COL_SPLITS = (512, 1024, 1536, 2048, 2560, 3072, 3200, 3264, 3328, 3392)
D_IN_PROJ = 3 * D_CONV + 3 * D_RWKV + D_GATE_LORA + 2 * D_DECAY_LORA + 2 * D_AAA_LORA
N_KEYS = 128
N_EXPERTS = N_KEYS * N_KEYS
PEER_HEADS = 8
PEER_TOPK = 16
D_QUERY = 256
D_HALF = D_QUERY // 2
PEER_CHUNK = 128
NORM_EPS = 1e-6
LNX_EPS = 64e-5

kernel_name = 'hybrid_conv_rwkv7_peer_dit_layer'


def _rmsnorm(x, g):
    xf = x.astype(jnp.float32)
    xf = xf * lax.rsqrt(jnp.mean(xf * xf, axis=-1, keepdims=True) + NORM_EPS)
    return xf.astype(x.dtype) * g


def _modulate(h, shift, scale):
    return h * (1 + scale) + shift


def _neighbours(z):
    pad = [(0, 0)] * (z.ndim - 2) + [(1, 1), (0, 0)]
    zp = jnp.pad(z, pad)
    return zp[..., :-2, :], zp[..., 2:, :]


def _conv3(z, w):
    prev, nxt = _neighbours(z)
    return w[0] * prev + w[1] * z + w[2] * nxt


def _token_shift(z, mu_prev, mu_next):
    prev, nxt = _neighbours(z)
    return z + mu_prev * (prev - z) + mu_next * (nxt - z)


def _wkv_scan(s0, r, w, k, v, kk, a, reverse):
    xs = tuple(jnp.moveaxis(z, 1, 0) for z in (r, w, k, v, kk, a))

    def step(s, inp):
        r_t, w_t, k_t, v_t, kk_t, a_t = inp
        sa = jnp.einsum('bhvk,bhk->bhv', s, kk_t)
        s = (s * w_t[:, :, None, :]
             - sa[..., :, None] * (kk_t * a_t)[:, :, None, :]
             + v_t[..., :, None] * k_t[:, :, None, :])
        return s, jnp.einsum('bhvk,bhk->bhv', s, r_t)

    s_fin, ys = lax.scan(step, s0, xs, reverse=reverse)
    return s_fin, jnp.moveaxis(ys, 0, 1)


def _token_mixer(h, s0_fwd, s0_bwd, on_grid, with_output, w_in, w_out, conv_w, tshift_mu,
                 decay_w0, decay_w2, iclr_a0, iclr_a2, gate_w2, k_k, k_a, r_k, lnx_g, lnx_b):
    bsz, length, _ = h.shape
    (xb, xc_gate, xc, r, k, v, g_lo, wf_lo, wb_lo, af_lo, ab_lo) = jnp.split(h @ w_in, COL_SPLITS, axis=-1)

    def heads(z):
        return z.astype(jnp.float32).reshape(bsz, length, N_RWKV_HEADS, HEAD_SIZE)

    r = _token_shift(r, tshift_mu[0, 0], tshift_mu[1, 0])
    k = _token_shift(k, tshift_mu[0, 1], tshift_mu[1, 1])
    v = _token_shift(v, tshift_mu[0, 2], tshift_mu[1, 2])
    r_h, k_h, v_h = heads(r), heads(k), heads(v)
    kk = k_h * k_k.astype(jnp.float32).reshape(N_RWKV_HEADS, HEAD_SIZE)
    kk = kk * lax.rsqrt(jnp.sum(kk * kk, axis=-1, keepdims=True) + 1e-12)
    k_a_h = k_a.astype(jnp.float32).reshape(N_RWKV_HEADS, HEAD_SIZE)
    finals, outs = [], []
    for d, (lo_w, lo_a, s0) in enumerate(((wf_lo, af_lo, s0_fwd), (wb_lo, ab_lo, s0_bwd))):
        logw = -jax.nn.softplus(-(decay_w0[d] + jnp.tanh(lo_w) @ decay_w2[d]).astype(jnp.float32)) - 0.5
        decay = heads(jnp.exp(-jnp.exp(logw)))
        a = heads(jax.nn.sigmoid(iclr_a0[d] + lo_a @ iclr_a2[d]))
        k_d = k_h * (1 + (a - 1) * k_a_h)
        s_fin, y_d = _wkv_scan(s0, r_h, decay, k_d, v_h, kk, a, reverse=(d == 1))
        finals.append(s_fin)
        outs.append(y_d)
    if not with_output:
        return None, finals[0], finals[1]
    y = outs[0] + outs[1]
    mu = jnp.mean(y, axis=-1, keepdims=True)
    var = jnp.mean(jnp.square(y - mu), axis=-1, keepdims=True)
    y = ((y - mu) * lax.rsqrt(var + LNX_EPS)).reshape(bsz, length, D_RWKV) * lnx_g + lnx_b
    bonus = jnp.sum(r_h * k_h * r_k.astype(jnp.float32), axis=-1, keepdims=True) * v_h
    y = (y + bonus.reshape(bsz, length, D_RWKV)).astype(h.dtype)
    y_rwkv = y * (jax.nn.sigmoid(g_lo) @ gate_w2)

    z = xc_gate * xc
    if on_grid:
        rows = length // GRID_W
        z = _conv3(z.reshape(bsz, rows, GRID_W, D_CONV), conv_w).reshape(bsz, length, D_CONV)
    else:
        z = _conv3(z, conv_w)
    y_conv = xb * z

    out = jnp.concatenate([y_conv, y_rwkv], axis=-1) @ w_out
    return out, finals[0], finals[1]


def _peer(h, wq, subkeys, u, v):
    bsz, length, dim = h.shape
    chunks = h.reshape(-1, PEER_CHUNK, dim)

    def chunk_fn(hc):
        q = (hc @ wq).reshape(PEER_CHUNK, PEER_HEADS, 2, D_HALF)
        s = jnp.einsum('chsd,hskd->chsk', q, subkeys)
        vals, idx = lax.top_k(s, PEER_TOPK)
        cand = vals[:, :, 0, :, None] + vals[:, :, 1, None, :]
        cand_idx = idx[:, :, 0, :, None] * N_KEYS + idx[:, :, 1, None, :]
        sc, pos = lax.top_k(cand.reshape(PEER_CHUNK, PEER_HEADS, PEER_TOPK * PEER_TOPK), PEER_TOPK)
        ex = jnp.take_along_axis(cand_idx.reshape(PEER_CHUNK, PEER_HEADS, PEER_TOPK * PEER_TOPK), pos, axis=-1)
        g = jax.nn.softmax(sc.astype(jnp.float32), axis=-1).astype(hc.dtype)
        act = jax.nn.gelu(jnp.einsum('chkd,cd->chk', u[ex], hc), approximate=False)
        return jnp.einsum('chk,chkd->cd', g * act, v[ex])

    return lax.map(chunk_fn, chunks).reshape(bsz, length, dim)


def setup_inputs(seed: int = 0) -> dict:
    key = jax.random.key(seed)
    ks = jax.random.split(key, 25)
    L, D = DEPTH, D_MODEL

    def nrm(k, shape, s):
        return jax.random.normal(k, shape, jnp.float32) * s

    return {
        'x': nrm(ks[0], (BATCH, SEQ, D), 1.0),
        'c': nrm(ks[1], (BATCH, D), 1.0),
        'ctx': nrm(ks[2], (BATCH, CTX_LEN, D), 1.0),
        'c_ctx': nrm(ks[3], (D,), 1.0),
        'w_mod': nrm(ks[4], (L, D, 6 * D), 0.5 * D ** -0.5),
        'b_mod': nrm(ks[5], (L, 6 * D), 0.01),
        'norm_gains': 1.0 + nrm(ks[6], (L, 4, D), 0.05),
        'w_in': nrm(ks[7], (L, D, D_IN_PROJ), D ** -0.5),
        'w_out': nrm(ks[8], (L, D_MIX, D), D_MIX ** -0.5),
        'conv_w': nrm(ks[9], (L, CONV_W, D_CONV), CONV_W ** -0.5),
        'tshift_mu': jax.random.uniform(ks[10], (L, 2, 3, D_RWKV), jnp.float32, 0.0, 0.5),
        'decay_w0': jax.random.uniform(ks[11], (L, 2, D_RWKV), jnp.float32, -6.0, 0.0),
        'decay_w2': nrm(ks[12], (L, 2, D_DECAY_LORA, D_RWKV), 0.5 * D_DECAY_LORA ** -0.5),
        'iclr_a0': nrm(ks[13], (L, 2, D_RWKV), 0.5),
        'iclr_a2': nrm(ks[14], (L, 2, D_AAA_LORA, D_RWKV), 0.5 * D_AAA_LORA ** -0.5),
        'gate_w2': nrm(ks[15], (L, D_GATE_LORA, D_RWKV), D_GATE_LORA ** -0.5),
        'k_k': 0.85 + nrm(ks[16], (L, D_RWKV), 0.05),
        'k_a': 1.0 + nrm(ks[17], (L, D_RWKV), 0.05),
        'r_k': nrm(ks[18], (L, N_RWKV_HEADS, HEAD_SIZE), 0.1),
        'lnx_g': 1.0 + nrm(ks[19], (L, D_RWKV), 0.05),
        'lnx_b': nrm(ks[20], (L, D_RWKV), 0.01),
        'peer_wq': nrm(ks[21], (L, D, PEER_HEADS * D_QUERY), D ** -0.5),
        'peer_subkeys': nrm(ks[22], (L, PEER_HEADS, 2, N_KEYS, D_HALF), D_HALF ** -0.5),
        'peer_u': nrm(ks[23], (L, N_EXPERTS, D), D ** -0.5),
        'peer_v': nrm(ks[24], (L, N_EXPERTS, D), D ** -0.5),
    }


def reference(x, c, ctx, c_ctx, w_mod, b_mod, norm_gains, w_in, w_out, conv_w, tshift_mu,
              decay_w0, decay_w2, iclr_a0, iclr_a2, gate_w2, k_k, k_a, r_k, lnx_g, lnx_b,
              peer_wq, peer_subkeys, peer_u, peer_v):
    bsz = x.shape[0]
    s_zero = jnp.zeros((bsz, N_RWKV_HEADS, HEAD_SIZE, HEAD_SIZE), jnp.float32)
    for l in range(DEPTH):
        last = l == DEPTH - 1
        mp = dict(w_in=w_in[l], w_out=w_out[l], conv_w=conv_w[l], tshift_mu=tshift_mu[l],
                  decay_w0=decay_w0[l], decay_w2=decay_w2[l], iclr_a0=iclr_a0[l], iclr_a2=iclr_a2[l],
                  gate_w2=gate_w2[l], k_k=k_k[l], k_a=k_a[l], r_k=r_k[l], lnx_g=lnx_g[l], lnx_b=lnx_b[l])
        ng = norm_gains[l]
        mod_lat = (jax.nn.silu(c) @ w_mod[l] + b_mod[l])[:, None, :]
        sh1, sc1, gt1, sh2, sc2, gt2 = jnp.split(mod_lat, 6, axis=-1)
        mod_ctx = jax.nn.silu(c_ctx) @ w_mod[l] + b_mod[l]
        csh1, csc1, cgt1, csh2, csc2, cgt2 = jnp.split(mod_ctx, 6, axis=-1)

        hc = _modulate(_rmsnorm(ctx, ng[0]), csh1, csc1)
        yc, st_fwd, st_bwd = _token_mixer(hc, s_zero, s_zero, False, not last, **mp)
        hx = _modulate(_rmsnorm(x, ng[0]), sh1, sc1)
        yx, _, _ = _token_mixer(hx, st_fwd, st_bwd, True, True, **mp)
        x = x + gt1 * _rmsnorm(yx, ng[1])
        if not last:
            ctx = ctx + cgt1 * _rmsnorm(yc, ng[1])

        hx = _modulate(_rmsnorm(x, ng[2]), sh2, sc2)
        x = x + gt2 * _rmsnorm(_peer(hx, peer_wq[l], peer_subkeys[l], peer_u[l], peer_v[l]), ng[3])
        if not last:
            hc = _modulate(_rmsnorm(ctx, ng[2]), csh2, csc2)
            ctx = ctx + cgt2 * _rmsnorm(_peer(hc, peer_wq[l], peer_subkeys[l], peer_u[l], peer_v[l]), ng[3])
    return x
```

```python
import functools
import math

import jax
import jax.numpy as jnp
from jax import lax
from jax.experimental import pallas as pl
from jax.experimental.pallas import tpu as pltpu

F32 = jnp.float32
BF16 = jnp.bfloat16
HI = lax.Precision.HIGHEST

NORM_EPS = 1e-6
LNX_EPS = 64e-5
HEAD = 64
NHEAD = 8
DR = HEAD * NHEAD
GRID_W = 64
CHUNK = 64
TOPK = 16
NKEYS = 128
PEER_HEADS = 8
NPAIR = PEER_HEADS * TOPK
TM = 256
TK = 128
VMEM_LIMIT = 56 * 1024 * 1024


def _cparams(sem):
    return pltpu.CompilerParams(dimension_semantics=sem, vmem_limit_bytes=VMEM_LIMIT)


def _const_spec(shape):
    nd = len(shape)
    return pl.BlockSpec(shape, lambda *_: (0,) * nd)


def _rms(x):
    return x * lax.rsqrt(jnp.mean(x * x, axis=-1, keepdims=True) + NORM_EPS)


def _mod_kernel(c_ref, w_ref, b_ref, o_ref):
    c = c_ref[...]
    s = c * jax.nn.sigmoid(c)
    o_ref[...] = jnp.dot(s.astype(BF16), w_ref[...].astype(BF16), preferred_element_type=F32) + b_ref[...]


def _mod_call(cin, w, b):
    rows, d = cin.shape
    n = w.shape[1]
    tn = 1024
    return pl.pallas_call(
        _mod_kernel,
        grid=(n // tn,),
        in_specs=[_const_spec((rows, d)), pl.BlockSpec((d, tn), lambda j: (0, j)), pl.BlockSpec((1, tn), lambda j: (0, j))],
        out_specs=pl.BlockSpec((rows, tn), lambda j: (0, j)),
        out_shape=jax.ShapeDtypeStruct((rows, n), F32),
        compiler_params=_cparams(("arbitrary",)),
        name="mod",
    )(cin, w, b)


def _prep_kernel(nt, xm_ref, xp_ref, xn_ref, sh_ref, sc_ref, g0_ref, win_ref, mu_ref, kk_ref, ka_ref, rk_ref,
                 w0_ref, w2_ref, a0_ref, a2_ref, gw2_ref, cw_ref, gsum_ref,
                 r_o, kap_o, v_o, kdf_o, kdb_o, akf_o, akb_o, lwf_o, lwb_o, yc_o, gate_o, bon_o):
    i = pl.program_id(1)
    g0 = g0_ref[...]
    sh = sh_ref[0, 0]
    sc = sc_ref[0, 0]

    def norm_mod(x):
        return (_rms(x) * g0 * (1.0 + sc) + sh).astype(BF16)

    hb = norm_mod(xm_ref[0])
    halo = norm_mod(jnp.concatenate([xp_ref[0], xn_ref[0]], axis=0))

    p_conv = jnp.dot(hb, win_ref[:, 0:3 * DR], preferred_element_type=F32)
    p_rkv = jnp.dot(hb, win_ref[:, 3 * DR:6 * DR], preferred_element_type=F32)
    p_lo = jnp.dot(hb, win_ref[:, 6 * DR:], preferred_element_type=F32)
    p_halo = jnp.dot(halo, win_ref[:, 3 * DR:6 * DR], preferred_element_type=F32)

    has_prev = (i >= 2).astype(F32)
    has_next = jnp.logical_and(i >= 1, i <= nt - 2).astype(F32)
    first = p_halo[7:8] * has_prev
    last = p_halo[8:9] * has_next

    rows = lax.broadcasted_iota(jnp.int32, (TM, 1), 0)

    def prev_of(z, row0):
        return jnp.where(rows == 0, row0, pltpu.roll(z, 1, axis=0))

    def next_of(z, rowl):
        return jnp.where(rows == TM - 1, rowl, pltpu.roll(z, TM - 1, axis=0))

    mu = mu_ref[...]

    def tshift(j):
        z = p_rkv[:, j * DR:(j + 1) * DR]
        zp = prev_of(z, first[:, j * DR:(j + 1) * DR])
        zn = next_of(z, last[:, j * DR:(j + 1) * DR])
        return z + mu[j:j + 1] * (zp - z) + mu[3 + j:4 + j] * (zn - z)

    r = tshift(0)
    k = tshift(1)
    v = tshift(2)

    gsum = gsum_ref[...]

    def group_sum(z):
        return jnp.dot(z, gsum, precision=HI, preferred_element_type=F32)

    kk = k * kk_ref[...]
    kap = kk * lax.rsqrt(group_sum(kk * kk) + 1e-12)
    bonus = group_sum(r * k * rk_ref[...]) * v

    lo_w = jnp.tanh(p_lo[:, 128:256])
    zw = jnp.dot(lo_w, w2_ref[...], precision=HI, preferred_element_type=F32) + w0_ref[...]
    lw = -math.exp(-0.5) * jax.nn.sigmoid(zw)
    za = jnp.dot(p_lo[:, 256:384], a2_ref[...], precision=HI, preferred_element_type=F32) + a0_ref[...]
    a = jax.nn.sigmoid(za)
    ka = ka_ref[...]

    r_o[0] = r
    kap_o[0] = kap
    v_o[0] = v
    for d, (kd_o, ak_o, lw_o) in enumerate(((kdf_o, akf_o, lwf_o), (kdb_o, akb_o, lwb_o))):
        a_d = a[:, d * DR:(d + 1) * DR]
        kd_o[0] = k * (1.0 + (a_d - 1.0) * ka)
        ak_o[0] = a_d * kap
        lw_o[0] = lw[:, d * DR:(d + 1) * DR]

    gate_o[0] = jnp.dot(jax.nn.sigmoid(p_lo[:, 0:128]), gw2_ref[...], precision=HI, preferred_element_type=F32)
    bon_o[0] = bonus

    z = p_conv[:, DR:2 * DR] * p_conv[:, 2 * DR:3 * DR]
    col = rows % GRID_W
    zp = jnp.where(col == 0, 0.0, pltpu.roll(z, 1, axis=0))
    zn = jnp.where(col == GRID_W - 1, 0.0, pltpu.roll(z, TM - 1, axis=0))
    cw = cw_ref[...]
    yc_o[0] = p_conv[:, 0:DR] * (cw[0:1] * zp + cw[1:2] * z + cw[2:3] * zn)


def _prep_call(xc, sh_all, sc_all, g0, win, mu6, k_k, k_a, r_k, w0, w2, a0, a2, gw2, cw, gsum):
    b, tt, d = xc.shape
    nt = tt // TM
    nb8 = tt // 8
    per8 = TM // 8
    row_spec = pl.BlockSpec((1, TM, d), lambda bb, i: (bb, i, 0))
    prev_spec = pl.BlockSpec((1, 8, d), lambda bb, i: (bb, jnp.maximum(i * per8 - 1, 0), 0))
    next_spec = pl.BlockSpec((1, 8, d), lambda bb, i: (bb, jnp.minimum((i + 1) * per8, nb8 - 1), 0))
    mod_spec = pl.BlockSpec((1, 1, 1, d), lambda bb, i: (bb, jnp.minimum(i, 1), 0, 0))
    consts = (g0, win, mu6, k_k, k_a, r_k, w0, w2, a0, a2, gw2, cw, gsum)
    out_spec = pl.BlockSpec((1, TM, DR), lambda bb, i: (bb, i, 0))
    out_sds = jax.ShapeDtypeStruct((b, tt, DR), F32)
    return pl.pallas_call(
        functools.partial(_prep_kernel, nt),
        grid=(b, nt),
        in_specs=[row_spec, prev_spec, next_spec, mod_spec, mod_spec] + [_const_spec(z.shape) for z in consts],
        out_specs=[out_spec] * 12,
        out_shape=[out_sds] * 12,
        compiler_params=_cparams(("arbitrary", "arbitrary")),
        name="prep",
    )(xc, xc, xc, sh_all, sc_all, *consts)


def _scan_chunk(r, kap, v, kd, ak, lw, h_ref, rev, bmask, bmask_f, tri):
    c = CHUNK
    t_idx = lax.broadcasted_iota(jnp.int32, (c, DR), 0)
    s_idx = lax.broadcasted_iota(jnp.int32, (c, DR), 1) % c
    if rev:
        strict = s_idx > t_idx
        incl = s_idx >= t_idx
    else:
        strict = s_idx < t_idx
        incl = s_idx <= t_idx
    eye = (s_idx == t_idx).astype(F32)

    cum = jnp.dot(tri, lw, precision=HI, preferred_element_type=F32)
    tot = cum[0:1] if rev else cum[c - 1:c]
    e_in = jnp.exp(cum)
    e_out = jnp.exp(-cum)
    e_rest = jnp.exp(tot - cum)
    kt = kap * jnp.exp(cum - lw)
    bh = ak * e_out
    kh = kd * e_out
    rt = r * e_in
    kbar = kd * e_rest
    bbar = ak * e_rest
    pc = jnp.exp(tot)

    def bd(z):
        return jnp.concatenate([z.astype(BF16)] * NHEAD, axis=0) * bmask

    def unbd(full):
        return (full * bmask_f).reshape(NHEAD, HEAD, DR).sum(axis=0)

    def mm(lhs, rhs):
        return jnp.dot(lhs.astype(BF16), rhs, preferred_element_type=F32)

    amat = lax.dot_general(jnp.concatenate([kt, rt], axis=0).astype(BF16),
                           jnp.concatenate([bd(bh), bd(kh)], axis=0),
                           (((1,), (1,)), ((), ())), preferred_element_type=F32)
    n0 = jnp.where(strict, -amat[:c, :DR], 0.0)
    akk = jnp.where(strict, amat[:c, DR:], 0.0)
    arb = jnp.where(incl, amat[c:, :DR], 0.0)
    ark = jnp.where(incl, amat[c:, DR:], 0.0)

    p = n0
    tinv = eye + n0
    p = mm(p, bd(p))
    for _ in range(int(math.log2(c)) - 2):
        out = mm(jnp.concatenate([p, tinv], axis=0), bd(p))
        tinv = tinv + out[c:]
        p = out[:c]
    tinv = tinv + mm(tinv, bd(p))

    bdv = bd(v)
    av = mm(jnp.concatenate([akk, ark], axis=0), bdv)
    wu = mm(tinv, jnp.concatenate([bd(kt), bd(av[:c])], axis=1))
    w = wu[:, :DR]
    uloc = wu[:, DR:]

    lhs_t = jnp.concatenate([kbar, -bbar], axis=0).astype(BF16)
    rhs_t = jnp.concatenate([jnp.concatenate([v, uloc], axis=0),
                             jnp.concatenate([jnp.zeros_like(w), w], axis=0)], axis=1).astype(BF16)
    full = lax.dot_general(lhs_t, rhs_t, (((0,), (0,)), ((), ())), preferred_element_type=F32)
    gcat = unbd(full[:, :DR])
    mcat = eye[:, :DR] * pc + unbd(full[:, DR:])

    bwu = mm(arb, jnp.concatenate([bd(w), bd(uloc)], axis=1))
    rprime = rt - bwu[:, :DR]
    yloc = av[c:] - bwu[:, DR:]

    h = h_ref[...]
    bdh = jnp.concatenate([h] * NHEAD, axis=0) * bmask_f
    out = jnp.dot(jnp.concatenate([rprime, mcat], axis=0), bdh, precision=HI, preferred_element_type=F32)
    h_ref[...] = out[c:] + gcat
    return yloc + out[:c]


def _scan_kernel(rf, kapf, vf, kdf, akf, lwf, rb, kapb, vb, kdb, akb, lwb, bm_ref, bmf_ref, trif_ref, trib_ref,
                 yf_o, yb_o, hf_s, hb_s):
    @pl.when(pl.program_id(1) == 0)
    def _():
        hf_s[...] = jnp.zeros_like(hf_s)
        hb_s[...] = jnp.zeros_like(hb_s)

    bm = bm_ref[...]
    bmf = bmf_ref[...]
    yf_o[0] = _scan_chunk(rf[0], kapf[0], vf[0], kdf[0], akf[0], lwf[0], hf_s, False, bm, bmf, trif_ref[...])
    yb_o[0] = _scan_chunk(rb[0], kapb[0], vb[0], kdb[0], akb[0], lwb[0], hb_s, True, bm, bmf, trib_ref[...])


def _scan_call(r, kap, v, kdf, kdb, akf, akb, lwf, lwb, n_ctx):
    b, tt, _ = r.shape
    nch = tt // CHUNK
    ncc = n_ctx // CHUNK
    hidx = jnp.arange(DR) // HEAD
    bmask_f = (hidx[:, None] == hidx[None, :]).astype(F32)
    ii = jnp.arange(CHUNK)
    tri_f = (ii[None, :] <= ii[:, None]).astype(F32)
    tri_b = (ii[None, :] >= ii[:, None]).astype(F32)

    def bwd_chunk(s):
        return jnp.where(s < ncc, ncc - 1 - s, nch - 1 - (s - ncc))

    fspec = pl.BlockSpec((1, CHUNK, DR), lambda bb, s: (bb, s, 0))
    bspec = pl.BlockSpec((1, CHUNK, DR), lambda bb, s: (bb, bwd_chunk(s), 0))
    consts = (bmask_f.astype(BF16), bmask_f, tri_f, tri_b)
    sds = jax.ShapeDtypeStruct((b, tt, DR), F32)
    return pl.pallas_call(
        _scan_kernel,
        grid=(b, nch),
        in_specs=[fspec] * 6 + [bspec] * 6 + [_const_spec(z.shape) for z in consts],
        out_specs=[fspec, bspec],
        out_shape=[sds, sds],
        scratch_shapes=[pltpu.VMEM((HEAD, DR), F32), pltpu.VMEM((HEAD, DR), F32)],
        compiler_params=_cparams(("arbitrary", "arbitrary")),
        name="scan",
    )(r, kap, v, kdf, akf, lwf, r, kap, v, kdb, akb, lwb, *consts)


def _post_kernel(yf_ref, yb_ref, bon_ref, gate_ref, yc_ref, x_ref, gt1_ref, sh2_ref, sc2_ref, g1_ref, g2_ref,
                 lg_ref, lb_ref, wout_ref, wq_ref, sk_ref, gmean_ref, x1_o, hx_o, st_o):
    gmean = gmean_ref[...]

    def group_mean(z):
        return jnp.dot(z, gmean, precision=HI, preferred_element_type=F32)

    y = yf_ref[0] + yb_ref[0]
    dlt = y - group_mean(y)
    yn = dlt * lax.rsqrt(group_mean(dlt * dlt) + LNX_EPS) * lg_ref[...] + lb_ref[...]
    y_rwkv = (yn + bon_ref[0]) * gate_ref[0]
    cat = jnp.concatenate([yc_ref[0], y_rwkv], axis=-1).astype(BF16)
    o = jnp.dot(cat, wout_ref[...], preferred_element_type=F32)
    x1 = x_ref[0] + gt1_ref[0] * (_rms(o) * g1_ref[...])
    x1_o[0] = x1
    hx = _rms(x1) * g2_ref[...] * (1.0 + sc2_ref[0]) + sh2_ref[0]
    hx_o[0] = hx
    q = jnp.dot(hx.astype(BF16), wq_ref[...], preferred_element_type=F32).astype(BF16)
    for l in range(2 * PEER_HEADS):
        ql = q[:, l * 128:(l + 1) * 128]
        st_o[l] = lax.dot_general(sk_ref[l], ql, (((1,), (1,)), ((), ())), preferred_element_type=F32)


def _post_call(yf, yb, bonus, gate, yconv, x, gt1, sh2, sc2, g1, g2, lnx_g, lnx_b, wout, wq, sk, gmean, n_ctx):
    b, t, d = x.shape
    nt = t // TM
    off = n_ctx // TM
    mix_spec = pl.BlockSpec((1, TM, DR), lambda bb, i: (bb, i + off, 0))
    x_spec = pl.BlockSpec((1, TM, d), lambda bb, i: (bb, i, 0))
    mod_spec = pl.BlockSpec((1, 1, d), lambda bb, i: (bb, 0, 0))
    consts = (g1, g2, lnx_g, lnx_b, wout, wq, sk, gmean)
    nl = 2 * PEER_HEADS
    return pl.pallas_call(
        _post_kernel,
        grid=(b, nt),
        in_specs=[mix_spec] * 5 + [x_spec] + [mod_spec] * 3 + [_const_spec(z.shape) for z in consts],
        out_specs=[x_spec, x_spec, pl.BlockSpec((nl, NKEYS, TM), lambda bb, i: (0, 0, bb * nt + i))],
        out_shape=[jax.ShapeDtypeStruct((b, t, d), F32), jax.ShapeDtypeStruct((b, t, d), F32),
                   jax.ShapeDtypeStruct((nl, NKEYS, b * t), F32)],
        compiler_params=_cparams(("arbitrary", "arbitrary")),
        name="post",
    )(yf, yb, bonus, gate, yconv, x, gt1, sh2, sc2, *consts)


def _top16(xv, order):
    big = jnp.int32(1 << 30)
    vals, sel = [], []
    for _ in range(TOPK):
        m = jnp.max(xv, axis=0, keepdims=True)
        am = jnp.min(jnp.where(xv == m, order, big), axis=0, keepdims=True)
        hit = order == am
        vals.append(m)
        sel.append(hit)
        xv = jnp.where(hit, -jnp.inf, xv)
    return vals, sel


def _topk_kernel(st_ref, ids_o, g_o):
    tk = st_ref.shape[-1]
    rowi = lax.broadcasted_iota(jnp.int32, (NKEYS, tk), 0)
    arow = lax.broadcasted_iota(jnp.int32, (8 * TOPK + 8, tk), 0)
    tail = arow >= 8 * TOPK
    stair = jnp.logical_or(tail, ((arow // TOPK) + 1) * ((arow % TOPK) + 1) <= TOPK)
    order2 = jnp.where(tail, 8 * TOPK + TOPK * (arow - 8 * TOPK), arow)
    ids_all, g_all = [], []
    for h in range(PEER_HEADS):
        tops = []
        for half in range(2):
            vals, sel = _top16(st_ref[2 * h + half], rowi)
            idx = [jnp.max(jnp.where(s, rowi, -1), axis=0, keepdims=True) for s in sel]
            tops.append((jnp.concatenate(vals, axis=0), jnp.concatenate(idx, axis=0)))
        (vi, ii), (vj, ij) = tops
        cand = jnp.concatenate([vi[a:a + 1] + vj for a in range(8)] + [vi[8:16] + vj[0:1]], axis=0)
        eid = jnp.concatenate([ii[a:a + 1] * NKEYS + ij for a in range(8)] + [ii[8:16] * NKEYS + ij[0:1]], axis=0)
        cand = jnp.where(stair, cand, -jnp.inf)
        vals, sel = _top16(cand, order2)
        sc = jnp.concatenate(vals, axis=0)
        ex = jnp.concatenate([jnp.max(jnp.where(s, eid, -1), axis=0, keepdims=True) for s in sel], axis=0)
        e = jnp.exp(sc - sc[0:1])
        g_all.append(e / jnp.sum(e, axis=0, keepdims=True))
        ids_all.append(ex)
    ids_o[...] = jnp.concatenate(ids_all, axis=0).T
    g_o[...] = jnp.concatenate(g_all, axis=0).T


def _topk_call(st):
    nl, nk, n = st.shape
    return pl.pallas_call(
        _topk_kernel,
        grid=(n // TK,),
        in_specs=[pl.BlockSpec((nl, nk, TK), lambda i: (0, 0, i))],
        out_specs=[pl.BlockSpec((TK, NPAIR), lambda i: (i, 0))] * 2,
        out_shape=[jax.ShapeDtypeStruct((n, NPAIR), jnp.int32), jax.ShapeDtypeStruct((n, NPAIR), F32)],
        compiler_params=_cparams(("arbitrary",)),
        name="topk",
    )(st)


def _gelu(x):
    return 0.5 * x * (1.0 + lax.erf(x * (1.0 / math.sqrt(2.0))))


def _peer_kernel(ids_ref, g_ref, hx_ref, x1_ref, gt2_ref, g3_ref, uv_hbm, o_ref, buf, sem, acc):
    tk, d = hx_ref.shape

    def rows_copy(e, slot, p, n):
        return pltpu.make_async_copy(uv_hbm.at[pl.ds(e, n)], buf.at[slot, pl.ds(p, n)], sem.at[slot])

    def issue(t, slot):
        def body(p, carry):
            rows_copy(ids_ref[t, p], slot, p, 1).start()
            return carry
        lax.fori_loop(0, NPAIR, body, 0, unroll=8)

    issue(0, 0)
    rowi = lax.broadcasted_iota(jnp.int32, (8, NPAIR), 0)

    def group(gi, carry):
        base = pl.multiple_of(gi * 8, 8)
        x8 = hx_ref[pl.ds(base, 8), :].astype(BF16)
        g8 = g_ref[pl.ds(base, 8), :]
        o8 = jnp.zeros((8, d), F32)
        for j in range(8):
            t = base + j
            slot = j % 2

            @pl.when(t + 1 < tk)
            def _():
                issue(t + 1, 1 - slot)

            rows_copy(0, slot, 0, NPAIR).wait()
            u = buf[slot, :, 0:d].astype(BF16)
            vv = buf[slot, :, d:2 * d].astype(BF16)
            act = lax.dot_general(x8, u, (((1,), (1,)), ((), ())), preferred_element_type=F32)
            w = jnp.where(rowi == j, g8 * _gelu(act), 0.0)
            o8 = o8 + jnp.dot(w.astype(BF16), vv, preferred_element_type=F32)
        acc[pl.ds(base, 8), :] = o8
        return carry

    lax.fori_loop(0, tk // 8, group, 0)
    o_ref[...] = x1_ref[...] + gt2_ref[0] * (_rms(acc[...]) * g3_ref[...])


def _peer_call(ids, gates, hx, x1, gt2, g3, uv, tokens_per_batch):
    n, d = hx.shape
    per_b = tokens_per_batch // TK
    tok_spec = pl.BlockSpec((TK, d), lambda i: (i, 0))
    return pl.pallas_call(
        _peer_kernel,
        grid=(n // TK,),
        in_specs=[pl.BlockSpec((TK, NPAIR), lambda i: (i, 0), memory_space=pltpu.SMEM),
                  pl.BlockSpec((TK, NPAIR), lambda i: (i, 0)),
                  tok_spec, tok_spec,
                  pl.BlockSpec((1, 1, d), lambda i: (i // per_b, 0, 0)),
                  _const_spec(g3.shape),
                  pl.BlockSpec(memory_space=pl.ANY)],
        out_specs=tok_spec,
        out_shape=jax.ShapeDtypeStruct((n, d), F32),
        scratch_shapes=[pltpu.VMEM((2, NPAIR, 2 * d), F32), pltpu.SemaphoreType.DMA((2,)), pltpu.VMEM((TK, d), F32)],
        compiler_params=_cparams(("arbitrary",)),
        name="peer",
    )(ids, gates, hx, x1, gt2, g3, uv)


def kernel(x, c, ctx, c_ctx, w_mod, b_mod, norm_gains, w_in, w_out, conv_w, tshift_mu, decay_w0, decay_w2, iclr_a0,
           iclr_a2, gate_w2, k_k, k_a, r_k, lnx_g, lnx_b, peer_wq, peer_subkeys, peer_u, peer_v):
    b, t, d = x.shape
    n_ctx = ctx.shape[1]
    assert w_mod.shape[0] == 1 and d == 2 * DR and t % TM == 0 and n_ctx == TM and b + 1 <= 8
    ng = norm_gains[0]

    cin = jnp.concatenate([c, c_ctx[None], jnp.zeros((8 - b - 1, d), F32)], axis=0)
    mod = _mod_call(cin, w_mod[0], b_mod[0][None])
    lat = mod[:b].reshape(b, 6, d)
    cm = mod[b].reshape(6, d)

    def both(j):
        return jnp.stack([jnp.broadcast_to(cm[j], (b, d)), lat[:, j]], axis=1)[:, :, None, :]

    xc = jnp.concatenate([ctx, x], axis=1)
    zeros = jnp.zeros((HEAD, DR), F32)
    w2 = jnp.concatenate([jnp.concatenate([decay_w2[0, 0], zeros], axis=1),
                          jnp.concatenate([zeros, decay_w2[0, 1]], axis=1)], axis=0)
    a2 = jnp.concatenate([jnp.concatenate([iclr_a2[0, 0], zeros], axis=1),
                          jnp.concatenate([zeros, iclr_a2[0, 1]], axis=1)], axis=0)
    hidx = jnp.arange(DR) // HEAD
    gsum = (hidx[:, None] == hidx[None, :]).astype(F32)

    feats = _prep_call(xc, both(0), both(1), ng[0][None], w_in[0].astype(BF16), tshift_mu[0].reshape(6, DR),
                       k_k[0][None], k_a[0][None], r_k[0].reshape(1, DR), decay_w0[0].reshape(1, 2 * DR), w2,
                       iclr_a0[0].reshape(1, 2 * DR), a2, gate_w2[0], conv_w[0], gsum)
    r, kap, v, kdf, kdb, akf, akb, lwf, lwb, yconv, gate, bonus = feats

    yf, yb = _scan_call(r, kap, v, kdf, kdb, akf, akb, lwf, lwb, n_ctx)

    sk = peer_subkeys[0].reshape(2 * PEER_HEADS, NKEYS, -1).astype(BF16)
    x1, hx, st = _post_call(yf, yb, bonus, gate, yconv, x, lat[:, 2][:, None], lat[:, 3][:, None], lat[:, 4][:, None],
                            ng[1][None], ng[2][None], lnx_g[0][None], lnx_b[0][None], w_out[0].astype(BF16),
                            peer_wq[0].astype(BF16), sk, gsum / HEAD, n_ctx)

    ids, gates = _topk_call(st)
    uv = jnp.concatenate([peer_u[0], peer_v[0]], axis=1)
    out = _peer_call(ids, gates, hx.reshape(b * t, d), x1.reshape(b * t, d), lat[:, 5][:, None], ng[3][None], uv, t)
    return out.reshape(b, t, d)
```

```python
import functools
import math

import jax
import jax.numpy as jnp
from jax import lax
from jax.experimental import pallas as pl
from jax.experimental.pallas import tpu as pltpu

F32 = jnp.float32
BF16 = jnp.bfloat16
HI = lax.Precision.HIGHEST

NORM_EPS = 1e-6
LNX_EPS = 64e-5
HEAD = 64
NHEAD = 8
DR = HEAD * NHEAD
GRID_W = 64
CHUNK = 64
TOPK = 16
NKEYS = 128
PEER_HEADS = 8
NPAIR = PEER_HEADS * TOPK
TM = 256
TK = 128
VMEM_LIMIT = 56 * 1024 * 1024


def _cparams(sem):
    return pltpu.CompilerParams(dimension_semantics=sem, vmem_limit_bytes=VMEM_LIMIT)


def _const_spec(shape):
    nd = len(shape)
    return pl.BlockSpec(shape, lambda *_: (0,) * nd)


def _rms(x):
    return x * lax.rsqrt(jnp.mean(x * x, axis=-1, keepdims=True) + NORM_EPS)


def _mod_kernel(c_ref, w_ref, b_ref, o_ref):
    c = c_ref[...]
    s = c * jax.nn.sigmoid(c)
    o_ref[...] = jnp.dot(s.astype(BF16), w_ref[...].astype(BF16), preferred_element_type=F32) + b_ref[...]


def _mod_call(cin, w, b):
    rows, d = cin.shape
    n = w.shape[1]
    tn = 1024
    return pl.pallas_call(
        _mod_kernel,
        grid=(n // tn,),
        in_specs=[_const_spec((rows, d)), pl.BlockSpec((d, tn), lambda j: (0, j)), pl.BlockSpec((1, tn), lambda j: (0, j))],
        out_specs=pl.BlockSpec((rows, tn), lambda j: (0, j)),
        out_shape=jax.ShapeDtypeStruct((rows, n), F32),
        compiler_params=_cparams(("arbitrary",)),
        name="mod",
    )(cin, w, b)


def _prep_kernel(nt, xm_ref, xp_ref, xn_ref, sh_ref, sc_ref, g0_ref, win_ref, mu_ref, kk_ref, ka_ref, rk_ref,
                 w0_ref, w2_ref, a0_ref, a2_ref, gw2_ref, cw_ref, gsum_ref,
                 r_o, kap_o, v_o, kdf_o, kdb_o, akf_o, akb_o, lwf_o, lwb_o, yc_o, gate_o, bon_o):
    i = pl.program_id(1)
    g0 = g0_ref[...]
    sh = sh_ref[0, 0]
    sc = sc_ref[0, 0]

    def norm_mod(x):
        return (_rms(x) * g0 * (1.0 + sc) + sh).astype(BF16)

    hb = norm_mod(xm_ref[0])
    halo = norm_mod(jnp.concatenate([xp_ref[0], xn_ref[0]], axis=0))

    p_conv = jnp.dot(hb, win_ref[:, 0:3 * DR], preferred_element_type=F32)
    p_rkv = jnp.dot(hb, win_ref[:, 3 * DR:6 * DR], preferred_element_type=F32)
    p_lo = jnp.dot(hb, win_ref[:, 6 * DR:], preferred_element_type=F32)
    p_halo = jnp.dot(halo, win_ref[:, 3 * DR:6 * DR], preferred_element_type=F32)

    has_prev = (i >= 2).astype(F32)
    has_next = jnp.logical_and(i >= 1, i <= nt - 2).astype(F32)
    first = p_halo[7:8] * has_prev
    last = p_halo[8:9] * has_next

    rows = lax.broadcasted_iota(jnp.int32, (TM, 1), 0)

    def prev_of(z, row0):
        return jnp.where(rows == 0, row0, pltpu.roll(z, 1, axis=0))

    def next_of(z, rowl):
        return jnp.where(rows == TM - 1, rowl, pltpu.roll(z, TM - 1, axis=0))

    mu = mu_ref[...]

    def tshift(j):
        z = p_rkv[:, j * DR:(j + 1) * DR]
        zp = prev_of(z, first[:, j * DR:(j + 1) * DR])
        zn = next_of(z, last[:, j * DR:(j + 1) * DR])
        return z + mu[j:j + 1] * (zp - z) + mu[3 + j:4 + j] * (zn - z)

    r = tshift(0)
    k = tshift(1)
    v = tshift(2)

    gsum = gsum_ref[...]

    def group_sum(z):
        return jnp.dot(z, gsum, precision=HI, preferred_element_type=F32)

    kk = k * kk_ref[...]
    kap = kk * lax.rsqrt(group_sum(kk * kk) + 1e-12)
    bonus = group_sum(r * k * rk_ref[...]) * v

    lo_w = jnp.tanh(p_lo[:, 128:256])
    zw = jnp.dot(lo_w, w2_ref[...], precision=HI, preferred_element_type=F32) + w0_ref[...]
    lw = -math.exp(-0.5) * jax.nn.sigmoid(zw)
    za = jnp.dot(p_lo[:, 256:384], a2_ref[...], precision=HI, preferred_element_type=F32) + a0_ref[...]
    a = jax.nn.sigmoid(za)
    ka = ka_ref[...]

    r_o[0] = r
    kap_o[0] = kap
    v_o[0] = v
    for d, (kd_o, ak_o, lw_o) in enumerate(((kdf_o, akf_o, lwf_o), (kdb_o, akb_o, lwb_o))):
        a_d = a[:, d * DR:(d + 1) * DR]
        kd_o[0] = k * (1.0 + (a_d - 1.0) * ka)
        ak_o[0] = a_d * kap
        lw_o[0] = lw[:, d * DR:(d + 1) * DR]

    gate_o[0] = jnp.dot(jax.nn.sigmoid(p_lo[:, 0:128]), gw2_ref[...], precision=HI, preferred_element_type=F32)
    bon_o[0] = bonus

    z = p_conv[:, DR:2 * DR] * p_conv[:, 2 * DR:3 * DR]
    col = rows % GRID_W
    zp = jnp.where(col == 0, 0.0, pltpu.roll(z, 1, axis=0))
    zn = jnp.where(col == GRID_W - 1, 0.0, pltpu.roll(z, TM - 1, axis=0))
    cw = cw_ref[...]
    yc_o[0] = p_conv[:, 0:DR] * (cw[0:1] * zp + cw[1:2] * z + cw[2:3] * zn)


def _prep_call(xc, sh_all, sc_all, g0, win, mu6, k_k, k_a, r_k, w0, w2, a0, a2, gw2, cw, gsum):
    b, tt, d = xc.shape
    nt = tt // TM
    nb8 = tt // 8
    per8 = TM // 8
    row_spec = pl.BlockSpec((1, TM, d), lambda bb, i: (bb, i, 0))
    prev_spec = pl.BlockSpec((1, 8, d), lambda bb, i: (bb, jnp.maximum(i * per8 - 1, 0), 0))
    next_spec = pl.BlockSpec((1, 8, d), lambda bb, i: (bb, jnp.minimum((i + 1) * per8, nb8 - 1), 0))
    mod_spec = pl.BlockSpec((1, 1, 1, d), lambda bb, i: (bb, jnp.minimum(i, 1), 0, 0))
    consts = (g0, win, mu6, k_k, k_a, r_k, w0, w2, a0, a2, gw2, cw, gsum)
    out_spec = pl.BlockSpec((1, TM, DR), lambda bb, i: (bb, i, 0))
    out_sds = jax.ShapeDtypeStruct((b, tt, DR), F32)
    return pl.pallas_call(
        functools.partial(_prep_kernel, nt),
        grid=(b, nt),
        in_specs=[row_spec, prev_spec, next_spec, mod_spec, mod_spec] + [_const_spec(z.shape) for z in consts],
        out_specs=[out_spec] * 12,
        out_shape=[out_sds] * 12,
        compiler_params=_cparams(("arbitrary", "arbitrary")),
        name="prep",
    )(xc, xc, xc, sh_all, sc_all, *consts)


def _scan_chunk(r, kap, v, kd, ak, lw, h_ref, rev, bmask, bmask_f, tri):
    c = CHUNK
    t_idx = lax.broadcasted_iota(jnp.int32, (c, DR), 0)
    s_idx = lax.broadcasted_iota(jnp.int32, (c, DR), 1) % c
    if rev:
        strict = s_idx > t_idx
        incl = s_idx >= t_idx
    else:
        strict = s_idx < t_idx
        incl = s_idx <= t_idx
    eye = (s_idx == t_idx).astype(F32)

    cum = jnp.dot(tri, lw, precision=HI, preferred_element_type=F32)
    tot = cum[0:1] if rev else cum[c - 1:c]
    e_in = jnp.exp(cum)
    e_out = jnp.exp(-cum)
    e_rest = jnp.exp(tot - cum)
    kt = kap * jnp.exp(cum - lw)
    bh = ak * e_out
    kh = kd * e_out
    rt = r * e_in
    kbar = kd * e_rest
    bbar = ak * e_rest
    pc = jnp.exp(tot)

    def bd(z):
        return jnp.concatenate([z.astype(BF16)] * NHEAD, axis=0) * bmask

    def unbd(full):
        return (full * bmask_f).reshape(NHEAD, HEAD, DR).sum(axis=0)

    def mm(lhs, rhs):
        return jnp.dot(lhs.astype(BF16), rhs, preferred_element_type=F32)

    amat = lax.dot_general(jnp.concatenate([kt, rt], axis=0).astype(BF16),
                           jnp.concatenate([bd(bh), bd(kh)], axis=0),
                           (((1,), (1,)), ((), ())), preferred_element_type=F32)
    n0 = jnp.where(strict, -amat[:c, :DR], 0.0)
    akk = jnp.where(strict, amat[:c, DR:], 0.0)
    arb = jnp.where(incl, amat[c:, :DR], 0.0)
    ark = jnp.where(incl, amat[c:, DR:], 0.0)

    p = n0
    tinv = eye + n0
    p = mm(p, bd(p))
    for _ in range(int(math.log2(c)) - 2):
        out = mm(jnp.concatenate([p, tinv], axis=0), bd(p))
        tinv = tinv + out[c:]
        p = out[:c]
    tinv = tinv + mm(tinv, bd(p))

    bdv = bd(v)
    av = mm(jnp.concatenate([akk, ark], axis=0), bdv)
    wu = mm(tinv, jnp.concatenate([bd(kt), bd(av[:c])], axis=1))
    w = wu[:, :DR]
    uloc = wu[:, DR:]

    lhs_t = jnp.concatenate([kbar, -bbar], axis=0).astype(BF16)
    rhs_t = jnp.concatenate([jnp.concatenate([v, uloc], axis=0),
                             jnp.concatenate([jnp.zeros_like(w), w], axis=0)], axis=1).astype(BF16)
    full = lax.dot_general(lhs_t, rhs_t, (((0,), (0,)), ((), ())), preferred_element_type=F32)
    gcat = unbd(full[:, :DR])
    mcat = eye[:, :DR] * pc + unbd(full[:, DR:])

    bwu = mm(arb, jnp.concatenate([bd(w), bd(uloc)], axis=1))
    rprime = rt - bwu[:, :DR]
    yloc = av[c:] - bwu[:, DR:]

    h = h_ref[...]
    bdh = jnp.concatenate([h] * NHEAD, axis=0) * bmask_f
    out = jnp.dot(jnp.concatenate([rprime, mcat], axis=0), bdh, precision=HI, preferred_element_type=F32)
    h_ref[...] = out[c:] + gcat
    return yloc + out[:c]


def _scan_kernel(rf, kapf, vf, kdf, akf, lwf, rb, kapb, vb, kdb, akb, lwb, bm_ref, bmf_ref, trif_ref, trib_ref,
                 yf_o, yb_o, hf_s, hb_s):
    @pl.when(pl.program_id(1) == 0)
    def _():
        hf_s[...] = jnp.zeros_like(hf_s)
        hb_s[...] = jnp.zeros_like(hb_s)

    bm = bm_ref[...]
    bmf = bmf_ref[...]
    yf_o[0] = _scan_chunk(rf[0], kapf[0], vf[0], kdf[0], akf[0], lwf[0], hf_s, False, bm, bmf, trif_ref[...])
    yb_o[0] = _scan_chunk(rb[0], kapb[0], vb[0], kdb[0], akb[0], lwb[0], hb_s, True, bm, bmf, trib_ref[...])


def _scan_call(r, kap, v, kdf, kdb, akf, akb, lwf, lwb, n_ctx):
    b, tt, _ = r.shape
    nch = tt // CHUNK
    ncc = n_ctx // CHUNK
    hidx = jnp.arange(DR) // HEAD
    bmask_f = (hidx[:, None] == hidx[None, :]).astype(F32)
    ii = jnp.arange(CHUNK)
    tri_f = (ii[None, :] <= ii[:, None]).astype(F32)
    tri_b = (ii[None, :] >= ii[:, None]).astype(F32)

    def bwd_chunk(s):
        return jnp.where(s < ncc, ncc - 1 - s, nch - 1 - (s - ncc))

    fspec = pl.BlockSpec((1, CHUNK, DR), lambda bb, s: (bb, s, 0))
    bspec = pl.BlockSpec((1, CHUNK, DR), lambda bb, s: (bb, bwd_chunk(s), 0))
    consts = (bmask_f.astype(BF16), bmask_f, tri_f, tri_b)
    sds = jax.ShapeDtypeStruct((b, tt, DR), F32)
    return pl.pallas_call(
        _scan_kernel,
        grid=(b, nch),
        in_specs=[fspec] * 6 + [bspec] * 6 + [_const_spec(z.shape) for z in consts],
        out_specs=[fspec, bspec],
        out_shape=[sds, sds],
        scratch_shapes=[pltpu.VMEM((HEAD, DR), F32), pltpu.VMEM((HEAD, DR), F32)],
        compiler_params=_cparams(("arbitrary", "arbitrary")),
        name="scan",
    )(r, kap, v, kdf, akf, lwf, r, kap, v, kdb, akb, lwb, *consts)


def _post_kernel(yf_ref, yb_ref, bon_ref, gate_ref, yc_ref, x_ref, gt1_ref, sh2_ref, sc2_ref, g1_ref, g2_ref,
                 lg_ref, lb_ref, wout_ref, wq_ref, sk_ref, gmean_ref, x1_o, hx_o, st_o):
    gmean = gmean_ref[...]

    def group_mean(z):
        return jnp.dot(z, gmean, precision=HI, preferred_element_type=F32)

    y = yf_ref[0] + yb_ref[0]
    dlt = y - group_mean(y)
    yn = dlt * lax.rsqrt(group_mean(dlt * dlt) + LNX_EPS) * lg_ref[...] + lb_ref[...]
    y_rwkv = (yn + bon_ref[0]) * gate_ref[0]
    cat = jnp.concatenate([yc_ref[0], y_rwkv], axis=-1).astype(BF16)
    o = jnp.dot(cat, wout_ref[...], preferred_element_type=F32)
    x1 = x_ref[0] + gt1_ref[0] * (_rms(o) * g1_ref[...])
    x1_o[0] = x1
    hx = _rms(x1) * g2_ref[...] * (1.0 + sc2_ref[0]) + sh2_ref[0]
    hx_o[0] = hx
    q = jnp.dot(hx.astype(BF16), wq_ref[...], preferred_element_type=F32).astype(BF16)
    for l in range(2 * PEER_HEADS):
        ql = q[:, l * 128:(l + 1) * 128]
        st_o[l] = lax.dot_general(sk_ref[l], ql, (((1,), (1,)), ((), ())), preferred_element_type=F32)


def _post_call(yf, yb, bonus, gate, yconv, x, gt1, sh2, sc2, g1, g2, lnx_g, lnx_b, wout, wq, sk, gmean, n_ctx):
    b, t, d = x.shape
    nt = t // TM
    off = n_ctx // TM
    mix_spec = pl.BlockSpec((1, TM, DR), lambda bb, i: (bb, i + off, 0))
    x_spec = pl.BlockSpec((1, TM, d), lambda bb, i: (bb, i, 0))
    mod_spec = pl.BlockSpec((1, 1, d), lambda bb, i: (bb, 0, 0))
    consts = (g1, g2, lnx_g, lnx_b, wout, wq, sk, gmean)
    nl = 2 * PEER_HEADS
    return pl.pallas_call(
        _post_kernel,
        grid=(b, nt),
        in_specs=[mix_spec] * 5 + [x_spec] + [mod_spec] * 3 + [_const_spec(z.shape) for z in consts],
        out_specs=[x_spec, x_spec, pl.BlockSpec((nl, NKEYS, TM), lambda bb, i: (0, 0, bb * nt + i))],
        out_shape=[jax.ShapeDtypeStruct((b, t, d), F32), jax.ShapeDtypeStruct((b, t, d), F32),
                   jax.ShapeDtypeStruct((nl, NKEYS, b * t), F32)],
        compiler_params=_cparams(("arbitrary", "arbitrary")),
        name="post",
    )(yf, yb, bonus, gate, yconv, x, gt1, sh2, sc2, *consts)


def _top16(xv, order):
    big = jnp.int32(1 << 30)
    vals, sel = [], []
    for _ in range(TOPK):
        m = jnp.max(xv, axis=0, keepdims=True)
        am = jnp.min(jnp.where(xv == m, order, big), axis=0, keepdims=True)
        hit = order == am
        vals.append(m)
        sel.append(hit)
        xv = jnp.where(hit, -jnp.inf, xv)
    return vals, sel


def _topk_kernel(st_ref, ids_o, g_o):
    tk = st_ref.shape[-1]
    rowi = lax.broadcasted_iota(jnp.int32, (NKEYS, tk), 0)
    arow = lax.broadcasted_iota(jnp.int32, (8 * TOPK + 8, tk), 0)
    tail = arow >= 8 * TOPK
    stair = jnp.logical_or(tail, ((arow // TOPK) + 1) * ((arow % TOPK) + 1) <= TOPK)
    order2 = jnp.where(tail, 8 * TOPK + TOPK * (arow - 8 * TOPK), arow)
    ids_all, g_all = [], []
    for h in range(PEER_HEADS):
        tops = []
        for half in range(2):
            vals, sel = _top16(st_ref[2 * h + half], rowi)
            idx = [jnp.max(jnp.where(s, rowi, -1), axis=0, keepdims=True) for s in sel]
            tops.append((jnp.concatenate(vals, axis=0), jnp.concatenate(idx, axis=0)))
        (vi, ii), (vj, ij) = tops
        cand = jnp.concatenate([vi[a:a + 1] + vj for a in range(8)] + [vi[8:16] + vj[0:1]], axis=0)
        eid = jnp.concatenate([ii[a:a + 1] * NKEYS + ij for a in range(8)] + [ii[8:16] * NKEYS + ij[0:1]], axis=0)
        cand = jnp.where(stair, cand, -jnp.inf)
        vals, sel = _top16(cand, order2)
        sc = jnp.concatenate(vals, axis=0)
        ex = jnp.concatenate([jnp.max(jnp.where(s, eid, -1), axis=0, keepdims=True) for s in sel], axis=0)
        e = jnp.exp(sc - sc[0:1])
        g_all.append(e / jnp.sum(e, axis=0, keepdims=True))
        ids_all.append(ex)
    ids_o[...] = jnp.concatenate(ids_all, axis=0).T
    g_o[...] = jnp.concatenate(g_all, axis=0).T


def _topk_call(st):
    nl, nk, n = st.shape
    return pl.pallas_call(
        _topk_kernel,
        grid=(n // TK,),
        in_specs=[pl.BlockSpec((nl, nk, TK), lambda i: (0, 0, i))],
        out_specs=[pl.BlockSpec((TK, NPAIR), lambda i: (i, 0))] * 2,
        out_shape=[jax.ShapeDtypeStruct((n, NPAIR), jnp.int32), jax.ShapeDtypeStruct((n, NPAIR), F32)],
        compiler_params=_cparams(("arbitrary",)),
        name="topk",
    )(st)


def _gelu(x):
    return 0.5 * x * (1.0 + lax.erf(x * (1.0 / math.sqrt(2.0))))


NSLOT = 4


def _peer_kernel(ids_ref, g_ref, hx_ref, x1_ref, gt2_ref, g3_ref, uv_hbm, o_ref, *scratch):
    bufs, sem, acc = scratch[:NSLOT], scratch[NSLOT], scratch[NSLOT + 1]
    tk, d = hx_ref.shape
    nc = d // 128

    def issue(t, slot):
        for p in range(NPAIR):
            e = ids_ref[t * NPAIR + p]
            pltpu.make_async_copy(uv_hbm.at[e], bufs[slot].at[:, p, :], sem.at[slot]).start(priority=p % 2)

    def wait(slot):
        pltpu.make_async_copy(bufs[slot], bufs[slot], sem.at[slot]).wait()

    rowi = lax.broadcasted_iota(jnp.int32, (8, NPAIR), 0)

    def group(base, static_tail):
        x8 = hx_ref[pl.ds(base, 8), :].astype(BF16)
        g8 = g_ref[pl.ds(base, 8), :]
        o8 = jnp.zeros((8, d), F32)
        for j in range(8):
            slot = j % NSLOT
            wait(slot)
            u = jnp.concatenate([bufs[slot][s] for s in range(nc)], axis=1).astype(BF16)
            vv = jnp.concatenate([bufs[slot][nc + s] for s in range(nc)], axis=1).astype(BF16)
            act = lax.dot_general(x8, u, (((1,), (1,)), ((), ())), preferred_element_type=F32)
            w = jnp.where(rowi == j, g8 * _gelu(act), 0.0)
            o8 = o8 + jnp.dot(w.astype(BF16), vv, preferred_element_type=F32)
            if not static_tail or j + NSLOT < 8:
                issue(base + j + NSLOT, slot)
        acc[pl.ds(base, 8), :] = o8

    for t in range(NSLOT):
        issue(t, t)

    def body(gi, carry):
        group(pl.multiple_of(gi * 8, 8), False)
        return carry

    lax.fori_loop(0, tk // 8 - 1, body, 0)
    group(tk - 8, True)
    o_ref[...] = x1_ref[...] + gt2_ref[0] * (_rms(acc[...]) * g3_ref[...])


def _peer_call(ids, gates, hx, x1, gt2, g3, uv, tokens_per_batch):
    n, d = hx.shape
    per_b = tokens_per_batch // TK
    tok_spec = pl.BlockSpec((TK, d), lambda i: (i, 0))
    return pl.pallas_call(
        _peer_kernel,
        grid=(n // TK,),
        in_specs=[pl.BlockSpec((TK * NPAIR,), lambda i: (i,), memory_space=pltpu.SMEM),
                  pl.BlockSpec((TK, NPAIR), lambda i: (i, 0)),
                  tok_spec, tok_spec,
                  pl.BlockSpec((1, 1, d), lambda i: (i // per_b, 0, 0)),
                  _const_spec(g3.shape),
                  pl.BlockSpec(memory_space=pl.ANY)],
        out_specs=tok_spec,
        out_shape=jax.ShapeDtypeStruct((n, d), F32),
        scratch_shapes=[pltpu.VMEM((2 * d // 128, NPAIR, 128), F32) for _ in range(NSLOT)]
        + [pltpu.SemaphoreType.DMA((NSLOT,)), pltpu.VMEM((TK, d), F32)],
        compiler_params=_cparams(("arbitrary",)),
        name="peer",
    )(ids, gates, hx, x1, gt2, g3, uv)


def kernel(x, c, ctx, c_ctx, w_mod, b_mod, norm_gains, w_in, w_out, conv_w, tshift_mu, decay_w0, decay_w2, iclr_a0,
           iclr_a2, gate_w2, k_k, k_a, r_k, lnx_g, lnx_b, peer_wq, peer_subkeys, peer_u, peer_v):
    b, t, d = x.shape
    n_ctx = ctx.shape[1]
    assert w_mod.shape[0] == 1 and d == 2 * DR and t % TM == 0 and n_ctx == TM and b + 1 <= 8
    ng = norm_gains[0]

    cin = jnp.concatenate([c, c_ctx[None], jnp.zeros((8 - b - 1, d), F32)], axis=0)
    mod = _mod_call(cin, w_mod[0], b_mod[0][None])
    lat = mod[:b].reshape(b, 6, d)
    cm = mod[b].reshape(6, d)

    def both(j):
        return jnp.stack([jnp.broadcast_to(cm[j], (b, d)), lat[:, j]], axis=1)[:, :, None, :]

    xc = jnp.concatenate([ctx, x], axis=1)
    zeros = jnp.zeros((HEAD, DR), F32)
    w2 = jnp.concatenate([jnp.concatenate([decay_w2[0, 0], zeros], axis=1),
                          jnp.concatenate([zeros, decay_w2[0, 1]], axis=1)], axis=0)
    a2 = jnp.concatenate([jnp.concatenate([iclr_a2[0, 0], zeros], axis=1),
                          jnp.concatenate([zeros, iclr_a2[0, 1]], axis=1)], axis=0)
    hidx = jnp.arange(DR) // HEAD
    gsum = (hidx[:, None] == hidx[None, :]).astype(F32)

    feats = _prep_call(xc, both(0), both(1), ng[0][None], w_in[0].astype(BF16), tshift_mu[0].reshape(6, DR),
                       k_k[0][None], k_a[0][None], r_k[0].reshape(1, DR), decay_w0[0].reshape(1, 2 * DR), w2,
                       iclr_a0[0].reshape(1, 2 * DR), a2, gate_w2[0], conv_w[0], gsum)
    r, kap, v, kdf, kdb, akf, akb, lwf, lwb, yconv, gate, bonus = feats

    yf, yb = _scan_call(r, kap, v, kdf, kdb, akf, akb, lwf, lwb, n_ctx)

    sk = peer_subkeys[0].reshape(2 * PEER_HEADS, NKEYS, -1).astype(BF16)
    x1, hx, st = _post_call(yf, yb, bonus, gate, yconv, x, lat[:, 2][:, None], lat[:, 3][:, None], lat[:, 4][:, None],
                            ng[1][None], ng[2][None], lnx_g[0][None], lnx_b[0][None], w_out[0].astype(BF16),
                            peer_wq[0].astype(BF16), sk, gsum / HEAD, n_ctx)

    ids, gates = _topk_call(st)
    uv = jnp.concatenate([peer_u[0], peer_v[0]], axis=1).reshape(-1, 2 * d // 128, 128)
    out = _peer_call(ids.reshape(-1), gates, hx.reshape(b * t, d), x1.reshape(b * t, d), lat[:, 5][:, None], ng[3][None], uv, t)
    return out.reshape(b, t, d)
```

```python
import functools
import math

import jax
import jax.numpy as jnp
from jax import lax
from jax.experimental import pallas as pl
from jax.experimental.pallas import tpu as pltpu

F32 = jnp.float32
BF16 = jnp.bfloat16
HI = lax.Precision.HIGHEST

NORM_EPS = 1e-6
LNX_EPS = 64e-5
HEAD = 64
NHEAD = 8
DR = HEAD * NHEAD
GRID_W = 64
CHUNK = 64
TOPK = 16
NKEYS = 128
PEER_HEADS = 8
NPAIR = PEER_HEADS * TOPK
TM = 256
TK = 128
SCAN_NB = 2
VMEM_LIMIT = 56 * 1024 * 1024


def _cparams(sem):
    return pltpu.CompilerParams(dimension_semantics=sem, vmem_limit_bytes=VMEM_LIMIT)


def _const_spec(shape):
    nd = len(shape)
    return pl.BlockSpec(shape, lambda *_: (0,) * nd)


def _rms(x):
    return x * lax.rsqrt(jnp.mean(x * x, axis=-1, keepdims=True) + NORM_EPS)


def _mod_kernel(c_ref, w_ref, b_ref, o_ref):
    c = c_ref[...]
    s = c * jax.nn.sigmoid(c)
    o_ref[...] = jnp.dot(s.astype(BF16), w_ref[...].astype(BF16), preferred_element_type=F32) + b_ref[...]


def _mod_call(cin, w, b):
    rows, d = cin.shape
    n = w.shape[1]
    tn = 1024
    return pl.pallas_call(
        _mod_kernel,
        grid=(n // tn,),
        in_specs=[_const_spec((rows, d)), pl.BlockSpec((d, tn), lambda j: (0, j)), pl.BlockSpec((1, tn), lambda j: (0, j))],
        out_specs=pl.BlockSpec((rows, tn), lambda j: (0, j)),
        out_shape=jax.ShapeDtypeStruct((rows, n), F32),
        compiler_params=_cparams(("arbitrary",)),
        name="mod",
    )(cin, w, b)


def _prep_kernel(nt, xm_ref, xp_ref, xn_ref, sh_ref, sc_ref, g0_ref, win_ref, mu_ref, kk_ref, ka_ref, rk_ref,
                 w0_ref, w2_ref, a0_ref, a2_ref, gw2_ref, cw_ref, gsum_ref,
                 r_o, kap_o, v_o, kdf_o, kdb_o, akf_o, akb_o, lwf_o, lwb_o, yc_o, gate_o, bon_o):
    i = pl.program_id(1)
    g0 = g0_ref[...]
    sh = sh_ref[0, 0]
    sc = sc_ref[0, 0]

    def norm_mod(x):
        return (_rms(x) * g0 * (1.0 + sc) + sh).astype(BF16)

    hb = norm_mod(xm_ref[0])
    halo = norm_mod(jnp.concatenate([xp_ref[0], xn_ref[0]], axis=0))

    p_conv = jnp.dot(hb, win_ref[:, 0:3 * DR], preferred_element_type=F32)
    p_rkv = jnp.dot(hb, win_ref[:, 3 * DR:6 * DR], preferred_element_type=F32)
    p_lo = jnp.dot(hb, win_ref[:, 6 * DR:], preferred_element_type=F32)
    p_halo = jnp.dot(halo, win_ref[:, 3 * DR:6 * DR], preferred_element_type=F32)

    has_prev = (i >= 2).astype(F32)
    has_next = jnp.logical_and(i >= 1, i <= nt - 2).astype(F32)
    first = p_halo[7:8] * has_prev
    last = p_halo[8:9] * has_next

    rows = lax.broadcasted_iota(jnp.int32, (TM, 1), 0)

    def prev_of(z, row0):
        return jnp.where(rows == 0, row0, pltpu.roll(z, 1, axis=0))

    def next_of(z, rowl):
        return jnp.where(rows == TM - 1, rowl, pltpu.roll(z, TM - 1, axis=0))

    mu = mu_ref[...]

    def tshift(j):
        z = p_rkv[:, j * DR:(j + 1) * DR]
        zp = prev_of(z, first[:, j * DR:(j + 1) * DR])
        zn = next_of(z, last[:, j * DR:(j + 1) * DR])
        return z + mu[j:j + 1] * (zp - z) + mu[3 + j:4 + j] * (zn - z)

    r = tshift(0)
    k = tshift(1)
    v = tshift(2)

    gsum = gsum_ref[...]

    def group_sum(z):
        return jnp.dot(z, gsum, precision=HI, preferred_element_type=F32)

    kk = k * kk_ref[...]
    kap = kk * lax.rsqrt(group_sum(kk * kk) + 1e-12)
    bonus = group_sum(r * k * rk_ref[...]) * v

    lo_w = jnp.tanh(p_lo[:, 128:256])
    zw = jnp.dot(lo_w, w2_ref[...], precision=HI, preferred_element_type=F32) + w0_ref[...]
    lw = -math.exp(-0.5) * jax.nn.sigmoid(zw)
    za = jnp.dot(p_lo[:, 256:384], a2_ref[...], precision=HI, preferred_element_type=F32) + a0_ref[...]
    a = jax.nn.sigmoid(za)
    ka = ka_ref[...]

    r_o[0] = r
    kap_o[0] = kap
    v_o[0] = v
    for d, (kd_o, ak_o, lw_o) in enumerate(((kdf_o, akf_o, lwf_o), (kdb_o, akb_o, lwb_o))):
        a_d = a[:, d * DR:(d + 1) * DR]
        kd_o[0] = k * (1.0 + (a_d - 1.0) * ka)
        ak_o[0] = a_d * kap
        lw_o[0] = lw[:, d * DR:(d + 1) * DR]

    gate_o[0] = jnp.dot(jax.nn.sigmoid(p_lo[:, 0:128]), gw2_ref[...], precision=HI, preferred_element_type=F32)
    bon_o[0] = bonus

    z = p_conv[:, DR:2 * DR] * p_conv[:, 2 * DR:3 * DR]
    col = rows % GRID_W
    zp = jnp.where(col == 0, 0.0, pltpu.roll(z, 1, axis=0))
    zn = jnp.where(col == GRID_W - 1, 0.0, pltpu.roll(z, TM - 1, axis=0))
    cw = cw_ref[...]
    yc_o[0] = p_conv[:, 0:DR] * (cw[0:1] * zp + cw[1:2] * z + cw[2:3] * zn)


def _prep_call(xc, sh_all, sc_all, g0, win, mu6, k_k, k_a, r_k, w0, w2, a0, a2, gw2, cw, gsum):
    b, tt, d = xc.shape
    nt = tt // TM
    nb8 = tt // 8
    per8 = TM // 8
    row_spec = pl.BlockSpec((1, TM, d), lambda bb, i: (bb, i, 0))
    prev_spec = pl.BlockSpec((1, 8, d), lambda bb, i: (bb, jnp.maximum(i * per8 - 1, 0), 0))
    next_spec = pl.BlockSpec((1, 8, d), lambda bb, i: (bb, jnp.minimum((i + 1) * per8, nb8 - 1), 0))
    mod_spec = pl.BlockSpec((1, 1, 1, d), lambda bb, i: (bb, jnp.minimum(i, 1), 0, 0))
    consts = (g0, win, mu6, k_k, k_a, r_k, w0, w2, a0, a2, gw2, cw, gsum)
    out_spec = pl.BlockSpec((1, TM, DR), lambda bb, i: (bb, i, 0))
    out_sds = jax.ShapeDtypeStruct((b, tt, DR), F32)
    return pl.pallas_call(
        functools.partial(_prep_kernel, nt),
        grid=(b, nt),
        in_specs=[row_spec, prev_spec, next_spec, mod_spec, mod_spec] + [_const_spec(z.shape) for z in consts],
        out_specs=[out_spec] * 12,
        out_shape=[out_sds] * 12,
        compiler_params=_cparams(("arbitrary", "arbitrary")),
        name="prep",
    )(xc, xc, xc, sh_all, sc_all, *consts)


def _scan_chunk(r, kap, v, kd, ak, lw, h, rev, bmask, bmask_f, tri):
    c = CHUNK
    t_idx = lax.broadcasted_iota(jnp.int32, (c, DR), 0)
    s_idx = lax.broadcasted_iota(jnp.int32, (c, DR), 1) % c
    if rev:
        strict = s_idx > t_idx
        incl = s_idx >= t_idx
    else:
        strict = s_idx < t_idx
        incl = s_idx <= t_idx
    eye = (s_idx == t_idx).astype(F32)

    cum = jnp.dot(tri, lw, precision=HI, preferred_element_type=F32)
    tot = cum[0:1] if rev else cum[c - 1:c]
    e_in = jnp.exp(cum)
    e_out = jnp.exp(-cum)
    e_rest = jnp.exp(tot - cum)
    kt = kap * jnp.exp(cum - lw)
    bh = ak * e_out
    kh = kd * e_out
    rt = r * e_in
    kbar = kd * e_rest
    bbar = ak * e_rest

    def bd(z):
        return jnp.concatenate([z.astype(BF16)] * NHEAD, axis=0) * bmask

    def unbd(full):
        return (full * bmask_f).reshape(NHEAD, HEAD, DR).sum(axis=0)

    def mm(lhs, rhs):
        return jnp.dot(lhs.astype(BF16), rhs, preferred_element_type=F32)

    amat = lax.dot_general(jnp.concatenate([kt, rt], axis=0).astype(BF16),
                           jnp.concatenate([bd(bh), bd(kh)], axis=0),
                           (((1,), (1,)), ((), ())), preferred_element_type=F32)
    n0 = jnp.where(strict, -amat[:c, :DR], 0.0)
    akk = jnp.where(strict, amat[:c, DR:], 0.0)
    arb = jnp.where(incl, amat[c:, :DR], 0.0)
    ark = jnp.where(incl, amat[c:, DR:], 0.0)

    p = n0
    tinv = eye + n0
    p = mm(p, bd(p))
    for _ in range(int(math.log2(c)) - 2):
        out = mm(jnp.concatenate([p, tinv], axis=0), bd(p))
        tinv = tinv + out[c:]
        p = out[:c]
    tinv = tinv + mm(tinv, bd(p))

    bdv = bd(v)
    av = mm(jnp.concatenate([akk, ark], axis=0), bdv)
    wu = mm(tinv, jnp.concatenate([bd(kt), bd(av[:c])], axis=1))
    w = wu[:, :DR]
    uloc = wu[:, DR:]

    lhs_t = jnp.concatenate([kbar, -bbar], axis=0).astype(BF16)
    rhs_t = jnp.concatenate([jnp.concatenate([v, uloc], axis=0),
                             jnp.concatenate([jnp.zeros_like(w), w], axis=0)], axis=1).astype(BF16)
    full = lax.dot_general(lhs_t, rhs_t, (((0,), (0,)), ((), ())), preferred_element_type=F32)
    gcat = unbd(full[:, :DR])
    corr = unbd(full[:, DR:])

    bwu = mm(arb, jnp.concatenate([bd(w), bd(uloc)], axis=1))
    rprime = rt - bwu[:, :DR]
    yloc = av[c:] - bwu[:, DR:]

    out = mm(jnp.concatenate([rprime, corr], axis=0), bd(h))
    pct = jnp.exp(jnp.broadcast_to(tot, (128, DR))).T
    low = lax.broadcasted_iota(jnp.int32, (HEAD, 128), 1) < HEAD
    pcb = jnp.concatenate([jnp.where(low, pct[2 * q * HEAD:(2 * q + 1) * HEAD], pct[(2 * q + 1) * HEAD:(2 * q + 2) * HEAD])
                           for q in range(NHEAD // 2)], axis=1)
    h_new = pcb * h + out[c:] + gcat
    return yloc + out[:c], h_new


def _scan_kernel(rf, kapf, vf, kdf, akf, lwf, rb, kapb, vb, kdb, akb, lwb, bm_ref, bmf_ref, trif_ref, trib_ref,
                 yf_o, yb_o, hf_s, hb_s):
    @pl.when(pl.program_id(1) == 0)
    def _():
        hf_s[...] = jnp.zeros_like(hf_s)
        hb_s[...] = jnp.zeros_like(hb_s)

    bm = bm_ref[...]
    bmf = bmf_ref[...]
    chains = []
    for i in range(rf.shape[0]):
        chains.append((yf_o, hf_s, i, (rf[i], kapf[i], vf[i], kdf[i], akf[i], lwf[i], hf_s[i], False, bm, bmf, trif_ref[...])))
        chains.append((yb_o, hb_s, i, (rb[i], kapb[i], vb[i], kdb[i], akb[i], lwb[i], hb_s[i], True, bm, bmf, trib_ref[...])))
    results = [_scan_chunk(*args) for (_, _, _, args) in chains]
    for (y_o, h_s, i, _), (y, h_new) in zip(chains, results):
        y_o[i] = y
        h_s[i] = h_new


def _scan_call(r, kap, v, kdf, kdb, akf, akb, lwf, lwb, n_ctx):
    b, tt, _ = r.shape
    nch = tt // CHUNK
    ncc = n_ctx // CHUNK
    hidx = jnp.arange(DR) // HEAD
    bmask_f = (hidx[:, None] == hidx[None, :]).astype(F32)
    ii = jnp.arange(CHUNK)
    tri_f = (ii[None, :] <= ii[:, None]).astype(F32)
    tri_b = (ii[None, :] >= ii[:, None]).astype(F32)

    def bwd_chunk(s):
        return jnp.where(s < ncc, ncc - 1 - s, nch - 1 - (s - ncc))

    nb = SCAN_NB if b % SCAN_NB == 0 else 1
    fspec = pl.BlockSpec((nb, CHUNK, DR), lambda bb, s: (bb, s, 0))
    bspec = pl.BlockSpec((nb, CHUNK, DR), lambda bb, s: (bb, bwd_chunk(s), 0))
    consts = (bmask_f.astype(BF16), bmask_f, tri_f, tri_b)
    sds = jax.ShapeDtypeStruct((b, tt, DR), F32)
    return pl.pallas_call(
        _scan_kernel,
        grid=(b // nb, nch),
        in_specs=[fspec] * 6 + [bspec] * 6 + [_const_spec(z.shape) for z in consts],
        out_specs=[fspec, bspec],
        out_shape=[sds, sds],
        scratch_shapes=[pltpu.VMEM((nb, HEAD, DR), F32), pltpu.VMEM((nb, HEAD, DR), F32)],
        compiler_params=_cparams(("arbitrary", "arbitrary")),
        name="scan",
    )(r, kap, v, kdf, akf, lwf, r, kap, v, kdb, akb, lwb, *consts)


def _post_kernel(yf_ref, yb_ref, bon_ref, gate_ref, yc_ref, x_ref, gt1_ref, sh2_ref, sc2_ref, g1_ref, g2_ref,
                 lg_ref, lb_ref, wout_ref, wq_ref, sk_ref, gmean_ref, x1_o, hx_o, st_o):
    gmean = gmean_ref[...]

    def group_mean(z):
        return jnp.dot(z, gmean, precision=HI, preferred_element_type=F32)

    y = yf_ref[0] + yb_ref[0]
    dlt = y - group_mean(y)
    yn = dlt * lax.rsqrt(group_mean(dlt * dlt) + LNX_EPS) * lg_ref[...] + lb_ref[...]
    y_rwkv = (yn + bon_ref[0]) * gate_ref[0]
    cat = jnp.concatenate([yc_ref[0], y_rwkv], axis=-1).astype(BF16)
    o = jnp.dot(cat, wout_ref[...], preferred_element_type=F32)
    x1 = x_ref[0] + gt1_ref[0] * (_rms(o) * g1_ref[...])
    x1_o[0] = x1
    hx = _rms(x1) * g2_ref[...] * (1.0 + sc2_ref[0]) + sh2_ref[0]
    hx_o[0] = hx
    q = jnp.dot(hx.astype(BF16), wq_ref[...], preferred_element_type=F32).astype(BF16)
    for l in range(2 * PEER_HEADS):
        ql = q[:, l * 128:(l + 1) * 128]
        st_o[l] = lax.dot_general(sk_ref[l], ql, (((1,), (1,)), ((), ())), preferred_element_type=F32)


def _post_call(yf, yb, bonus, gate, yconv, x, gt1, sh2, sc2, g1, g2, lnx_g, lnx_b, wout, wq, sk, gmean, n_ctx):
    b, t, d = x.shape
    nt = t // TM
    off = n_ctx // TM
    mix_spec = pl.BlockSpec((1, TM, DR), lambda bb, i: (bb, i + off, 0))
    x_spec = pl.BlockSpec((1, TM, d), lambda bb, i: (bb, i, 0))
    mod_spec = pl.BlockSpec((1, 1, d), lambda bb, i: (bb, 0, 0))
    consts = (g1, g2, lnx_g, lnx_b, wout, wq, sk, gmean)
    nl = 2 * PEER_HEADS
    return pl.pallas_call(
        _post_kernel,
        grid=(b, nt),
        in_specs=[mix_spec] * 5 + [x_spec] + [mod_spec] * 3 + [_const_spec(z.shape) for z in consts],
        out_specs=[x_spec, x_spec, pl.BlockSpec((nl, NKEYS, TM), lambda bb, i: (0, 0, bb * nt + i))],
        out_shape=[jax.ShapeDtypeStruct((b, t, d), F32), jax.ShapeDtypeStruct((b, t, d), F32),
                   jax.ShapeDtypeStruct((nl, NKEYS, b * t), F32)],
        compiler_params=_cparams(("arbitrary", "arbitrary")),
        name="post",
    )(yf, yb, bonus, gate, yconv, x, gt1, sh2, sc2, *consts)


def _top16(xv, order):
    big = jnp.int32(1 << 30)
    vals, picks, hits = [], [], []
    for _ in range(TOPK):
        m = jnp.max(xv, axis=0, keepdims=True)
        am = jnp.min(jnp.where(xv == m, order, big), axis=0, keepdims=True)
        hit = order == am
        vals.append(m)
        picks.append(am)
        hits.append(hit)
        xv = jnp.where(hit, -jnp.inf, xv)
    return vals, picks, hits


def _topk_kernel(st_ref, ids_o, g_o):
    tk = st_ref.shape[-1]
    rowi = lax.broadcasted_iota(jnp.int32, (NKEYS, tk), 0)
    crow = lax.broadcasted_iota(jnp.int32, (10 * 8, tk), 0)
    ca = jnp.where(crow < 16, 0, jnp.where(crow < 72, 1 + (crow - 16) // 8, crow - 64))
    cb = jnp.where(crow < 16, crow, jnp.where(crow < 72, (crow - 16) % 8, 0))
    stair = (ca + 1) * (cb + 1) <= TOPK
    order2 = ca * TOPK + cb
    ids_all, g_all = [], []
    for h in range(PEER_HEADS):
        tops = []
        for half in range(2):
            vals, picks, _ = _top16(st_ref[2 * h + half], rowi)
            tops.append((jnp.concatenate(vals, axis=0), jnp.concatenate(picks, axis=0)))
        (vi, ii), (vj, ij) = tops
        cand = jnp.concatenate([vi[0:1] + vj] + [vi[a:a + 1] + vj[0:8] for a in range(1, 8)] + [vi[8:16] + vj[0:1]],
                               axis=0)
        eid = jnp.concatenate([ii[0:1] * NKEYS + ij] + [ii[a:a + 1] * NKEYS + ij[0:8] for a in range(1, 8)]
                              + [ii[8:16] * NKEYS + ij[0:1]], axis=0)
        cand = jnp.where(stair, cand, -jnp.inf)
        vals, _, hits = _top16(cand, order2)
        sc = jnp.concatenate(vals, axis=0)
        ex = jnp.concatenate([jnp.max(jnp.where(s, eid, -1), axis=0, keepdims=True) for s in hits], axis=0)
        e = jnp.exp(sc - sc[0:1])
        g_all.append(e / jnp.sum(e, axis=0, keepdims=True))
        ids_all.append(ex)
    ids_o[...] = jnp.concatenate(ids_all, axis=0).T
    g_o[...] = jnp.concatenate(g_all, axis=0).T


def _topk_call(st):
    nl, nk, n = st.shape
    return pl.pallas_call(
        _topk_kernel,
        grid=(n // TK,),
        in_specs=[pl.BlockSpec((nl, nk, TK), lambda i: (0, 0, i))],
        out_specs=[pl.BlockSpec((TK, NPAIR), lambda i: (i, 0))] * 2,
        out_shape=[jax.ShapeDtypeStruct((n, NPAIR), jnp.int32), jax.ShapeDtypeStruct((n, NPAIR), F32)],
        compiler_params=_cparams(("arbitrary",)),
        name="topk",
    )(st)


def _gelu(x):
    return 0.5 * x * (1.0 + lax.erf(x * (1.0 / math.sqrt(2.0))))


NSLOT = 4


def _peer_kernel(ids_ref, g_ref, hx_ref, x1_ref, gt2_ref, g3_ref, uv_hbm, o_ref, *scratch):
    bufs, sem, acc = scratch[:NSLOT], scratch[NSLOT], scratch[NSLOT + 1]
    tk, d = hx_ref.shape
    nc = d // 128

    def issue(t, slot):
        for p in range(NPAIR):
            e = ids_ref[t * NPAIR + p]
            pltpu.make_async_copy(uv_hbm.at[e], bufs[slot].at[:, p, :], sem.at[slot]).start(priority=p % 2)

    def wait(slot):
        pltpu.make_async_copy(bufs[slot], bufs[slot], sem.at[slot]).wait()

    rowi = lax.broadcasted_iota(jnp.int32, (8, NPAIR), 0)

    def group(base, static_tail):
        x8 = hx_ref[pl.ds(base, 8), :].astype(BF16)
        g8 = g_ref[pl.ds(base, 8), :]
        o8 = jnp.zeros((8, d), F32)
        for j in range(8):
            slot = j % NSLOT
            wait(slot)
            u = jnp.concatenate([bufs[slot][s] for s in range(nc)], axis=1).astype(BF16)
            vv = jnp.concatenate([bufs[slot][nc + s] for s in range(nc)], axis=1).astype(BF16)
            act = lax.dot_general(x8, u, (((1,), (1,)), ((), ())), preferred_element_type=F32)
            w = jnp.where(rowi == j, g8 * _gelu(act), 0.0)
            o8 = o8 + jnp.dot(w.astype(BF16), vv, preferred_element_type=F32)
            if not static_tail or j + NSLOT < 8:
                issue(base + j + NSLOT, slot)
        acc[pl.ds(base, 8), :] = o8

    for t in range(NSLOT):
        issue(t, t)

    def body(gi, carry):
        group(pl.multiple_of(gi * 8, 8), False)
        return carry

    lax.fori_loop(0, tk // 8 - 1, body, 0)
    group(tk - 8, True)
    o_ref[...] = x1_ref[...] + gt2_ref[0] * (_rms(acc[...]) * g3_ref[...])


def _peer_call(ids, gates, hx, x1, gt2, g3, uv, tokens_per_batch):
    n, d = hx.shape
    per_b = tokens_per_batch // TK
    tok_spec = pl.BlockSpec((TK, d), lambda i: (i, 0))
    return pl.pallas_call(
        _peer_kernel,
        grid=(n // TK,),
        in_specs=[pl.BlockSpec((TK * NPAIR,), lambda i: (i,), memory_space=pltpu.SMEM),
                  pl.BlockSpec((TK, NPAIR), lambda i: (i, 0)),
                  tok_spec, tok_spec,
                  pl.BlockSpec((1, 1, d), lambda i: (i // per_b, 0, 0)),
                  _const_spec(g3.shape),
                  pl.BlockSpec(memory_space=pl.ANY)],
        out_specs=tok_spec,
        out_shape=jax.ShapeDtypeStruct((n, d), F32),
        scratch_shapes=[pltpu.VMEM((2 * d // 128, NPAIR, 128), F32) for _ in range(NSLOT)]
        + [pltpu.SemaphoreType.DMA((NSLOT,)), pltpu.VMEM((TK, d), F32)],
        compiler_params=_cparams(("arbitrary",)),
        name="peer",
    )(ids, gates, hx, x1, gt2, g3, uv)


def kernel(x, c, ctx, c_ctx, w_mod, b_mod, norm_gains, w_in, w_out, conv_w, tshift_mu, decay_w0, decay_w2, iclr_a0,
           iclr_a2, gate_w2, k_k, k_a, r_k, lnx_g, lnx_b, peer_wq, peer_subkeys, peer_u, peer_v):
    b, t, d = x.shape
    n_ctx = ctx.shape[1]
    assert w_mod.shape[0] == 1 and d == 2 * DR and t % TM == 0 and n_ctx == TM and b + 1 <= 8
    ng = norm_gains[0]

    cin = jnp.concatenate([c, c_ctx[None], jnp.zeros((8 - b - 1, d), F32)], axis=0)
    mod = _mod_call(cin, w_mod[0], b_mod[0][None])
    lat = mod[:b].reshape(b, 6, d)
    cm = mod[b].reshape(6, d)

    def both(j):
        return jnp.stack([jnp.broadcast_to(cm[j], (b, d)), lat[:, j]], axis=1)[:, :, None, :]

    xc = jnp.concatenate([ctx, x], axis=1)
    zeros = jnp.zeros((HEAD, DR), F32)
    w2 = jnp.concatenate([jnp.concatenate([decay_w2[0, 0], zeros], axis=1),
                          jnp.concatenate([zeros, decay_w2[0, 1]], axis=1)], axis=0)
    a2 = jnp.concatenate([jnp.concatenate([iclr_a2[0, 0], zeros], axis=1),
                          jnp.concatenate([zeros, iclr_a2[0, 1]], axis=1)], axis=0)
    hidx = jnp.arange(DR) // HEAD
    gsum = (hidx[:, None] == hidx[None, :]).astype(F32)

    feats = _prep_call(xc, both(0), both(1), ng[0][None], w_in[0].astype(BF16), tshift_mu[0].reshape(6, DR),
                       k_k[0][None], k_a[0][None], r_k[0].reshape(1, DR), decay_w0[0].reshape(1, 2 * DR), w2,
                       iclr_a0[0].reshape(1, 2 * DR), a2, gate_w2[0], conv_w[0], gsum)
    r, kap, v, kdf, kdb, akf, akb, lwf, lwb, yconv, gate, bonus = feats

    yf, yb = _scan_call(r, kap, v, kdf, kdb, akf, akb, lwf, lwb, n_ctx)

    sk = peer_subkeys[0].reshape(2 * PEER_HEADS, NKEYS, -1).astype(BF16)
    x1, hx, st = _post_call(yf, yb, bonus, gate, yconv, x, lat[:, 2][:, None], lat[:, 3][:, None], lat[:, 4][:, None],
                            ng[1][None], ng[2][None], lnx_g[0][None], lnx_b[0][None], w_out[0].astype(BF16),
                            peer_wq[0].astype(BF16), sk, gsum / HEAD, n_ctx)

    ids, gates = _topk_call(st)
    uv = jnp.concatenate([peer_u[0], peer_v[0]], axis=1).reshape(-1, 2 * d // 128, 128)
    out = _peer_call(ids.reshape(-1), gates, hx.reshape(b * t, d), x1.reshape(b * t, d), lat[:, 5][:, None], ng[3][None], uv, t)
    return out.reshape(b, t, d)
```

```python
import functools
import math

import jax
import jax.numpy as jnp
from jax import lax
from jax.experimental import pallas as pl
from jax.experimental.pallas import tpu as pltpu

F32 = jnp.float32
BF16 = jnp.bfloat16
HI = lax.Precision.HIGHEST

NORM_EPS = 1e-6
LNX_EPS = 64e-5
HEAD = 64
NHEAD = 8
DR = HEAD * NHEAD
GRID_W = 64
CHUNK = 64
TOPK = 16
NKEYS = 128
PEER_HEADS = 8
NPAIR = PEER_HEADS * TOPK
TM = 256
TK = 128
SCAN_NB = 2
VMEM_LIMIT = 56 * 1024 * 1024


def _cparams(sem):
    return pltpu.CompilerParams(dimension_semantics=sem, vmem_limit_bytes=VMEM_LIMIT)


def _const_spec(shape):
    nd = len(shape)
    return pl.BlockSpec(shape, lambda *_: (0,) * nd)


def _rms(x):
    return x * lax.rsqrt(jnp.mean(x * x, axis=-1, keepdims=True) + NORM_EPS)


def _dot_ones(z, ones):
    hi = z.astype(BF16)
    lo = (z - hi.astype(F32)).astype(BF16)
    both = jnp.dot(jnp.concatenate([hi, lo], axis=0), ones, preferred_element_type=F32)
    return both[:z.shape[0]] + both[z.shape[0]:]


def _mod_kernel(c_ref, w_ref, b_ref, o_ref):
    c = c_ref[...]
    s = c * jax.nn.sigmoid(c)
    o_ref[...] = jnp.dot(s.astype(BF16), w_ref[...].astype(BF16), preferred_element_type=F32) + b_ref[...]


def _mod_call(cin, w, b):
    rows, d = cin.shape
    n = w.shape[1]
    tn = 1024
    return pl.pallas_call(
        _mod_kernel,
        grid=(n // tn,),
        in_specs=[_const_spec((rows, d)), pl.BlockSpec((d, tn), lambda j: (0, j)), pl.BlockSpec((1, tn), lambda j: (0, j))],
        out_specs=pl.BlockSpec((rows, tn), lambda j: (0, j)),
        out_shape=jax.ShapeDtypeStruct((rows, n), F32),
        compiler_params=_cparams(("arbitrary",)),
        name="mod",
    )(cin, w, b)


def _prep_kernel(nt, ctx_ref, xm_ref, xp_ref, xn_ref, sh_ref, sc_ref, g0_ref, win_ref, mu_ref, kk_ref, ka_ref, rk_ref,
                 w0_ref, w2_ref, a0_ref, a2_ref, gw2_ref, cw_ref, gsum_ref,
                 r_o, kap_o, v_o, kdf_o, kdb_o, akf_o, akb_o, lwf_o, lwb_o, yc_o, gate_o, bon_o):
    i = pl.program_id(1)
    g0 = g0_ref[...]
    sh = sh_ref[0, 0]
    sc = sc_ref[0, 0]

    def norm_mod(x):
        return (_rms(x) * g0 * (1.0 + sc) + sh).astype(BF16)

    hb = norm_mod(jnp.where(i == 0, ctx_ref[0], xm_ref[0]))
    halo = norm_mod(jnp.concatenate([xp_ref[0], xn_ref[0]], axis=0))

    p_conv = jnp.dot(hb, win_ref[:, 0:3 * DR], preferred_element_type=F32)
    p_rkv = jnp.dot(hb, win_ref[:, 3 * DR:6 * DR], preferred_element_type=F32)
    p_lo = jnp.dot(hb, win_ref[:, 6 * DR:], preferred_element_type=F32)
    p_halo = jnp.dot(halo, win_ref[:, 3 * DR:6 * DR], preferred_element_type=F32)

    has_prev = (i >= 2).astype(F32)
    has_next = jnp.logical_and(i >= 1, i <= nt - 2).astype(F32)
    first = p_halo[7:8] * has_prev
    last = p_halo[8:9] * has_next

    rows = lax.broadcasted_iota(jnp.int32, (TM, 1), 0)

    def prev_of(z, row0):
        return jnp.where(rows == 0, row0, pltpu.roll(z, 1, axis=0))

    def next_of(z, rowl):
        return jnp.where(rows == TM - 1, rowl, pltpu.roll(z, TM - 1, axis=0))

    mu = mu_ref[...]

    def tshift(j):
        z = p_rkv[:, j * DR:(j + 1) * DR]
        zp = prev_of(z, first[:, j * DR:(j + 1) * DR])
        zn = next_of(z, last[:, j * DR:(j + 1) * DR])
        return z + mu[j:j + 1] * (zp - z) + mu[3 + j:4 + j] * (zn - z)

    r = tshift(0)
    k = tshift(1)
    v = tshift(2)

    gsum = gsum_ref[...]

    kk = k * kk_ref[...]
    sums = _dot_ones(jnp.concatenate([kk * kk, r * k * rk_ref[...]], axis=0), gsum)
    kap = kk * lax.rsqrt(sums[:TM] + 1e-12)
    bonus = sums[TM:] * v

    lo_w = jnp.tanh(p_lo[:, 128:256])
    zw = jnp.dot(lo_w, w2_ref[...], precision=HI, preferred_element_type=F32) + w0_ref[...]
    lw = -math.exp(-0.5) * jax.nn.sigmoid(zw)
    za = jnp.dot(p_lo[:, 256:384].astype(BF16), a2_ref[...], preferred_element_type=F32) + a0_ref[...]
    a = jax.nn.sigmoid(za)
    ka = ka_ref[...]

    r_o[0] = r
    kap_o[0] = kap
    v_o[0] = v
    for d, (kd_o, ak_o, lw_o) in enumerate(((kdf_o, akf_o, lwf_o), (kdb_o, akb_o, lwb_o))):
        a_d = a[:, d * DR:(d + 1) * DR]
        kd_o[0] = k * (1.0 + (a_d - 1.0) * ka)
        ak_o[0] = a_d * kap
        lw_o[0] = lw[:, d * DR:(d + 1) * DR]

    gate_o[0] = jnp.dot(jax.nn.sigmoid(p_lo[:, 0:128]).astype(BF16), gw2_ref[...], preferred_element_type=F32)
    bon_o[0] = bonus

    z = p_conv[:, DR:2 * DR] * p_conv[:, 2 * DR:3 * DR]
    col = rows % GRID_W
    zp = jnp.where(col == 0, 0.0, pltpu.roll(z, 1, axis=0))
    zn = jnp.where(col == GRID_W - 1, 0.0, pltpu.roll(z, TM - 1, axis=0))
    cw = cw_ref[...]
    yc_o[0] = p_conv[:, 0:DR] * (cw[0:1] * zp + cw[1:2] * z + cw[2:3] * zn)


def _prep_call(ctx, x, sh_all, sc_all, g0, win, mu6, k_k, k_a, r_k, w0, w2, a0, a2, gw2, cw, gsum):
    b, t, d = x.shape
    tt = t + ctx.shape[1]
    nt = tt // TM
    nb8 = t // 8
    per8 = TM // 8
    ctx_spec = pl.BlockSpec((1, TM, d), lambda bb, i: (bb, 0, 0))
    row_spec = pl.BlockSpec((1, TM, d), lambda bb, i: (bb, jnp.maximum(i - 1, 0), 0))
    prev_spec = pl.BlockSpec((1, 8, d), lambda bb, i: (bb, jnp.maximum((i - 1) * per8 - 1, 0), 0))
    next_spec = pl.BlockSpec((1, 8, d), lambda bb, i: (bb, jnp.clip(i * per8, 0, nb8 - 1), 0))
    mod_spec = pl.BlockSpec((1, 1, 1, d), lambda bb, i: (bb, jnp.minimum(i, 1), 0, 0))
    consts = (g0, win, mu6, k_k, k_a, r_k, w0, w2, a0, a2, gw2, cw, gsum)
    out_spec = pl.BlockSpec((1, TM, DR), lambda bb, i: (bb, i, 0))
    out_sds = jax.ShapeDtypeStruct((b, tt, DR), F32)
    return pl.pallas_call(
        functools.partial(_prep_kernel, nt),
        grid=(b, nt),
        in_specs=[ctx_spec, row_spec, prev_spec, next_spec, mod_spec, mod_spec] + [_const_spec(z.shape) for z in consts],
        out_specs=[out_spec] * 12,
        out_shape=[out_sds] * 12,
        compiler_params=_cparams(("arbitrary", "arbitrary")),
        name="prep",
    )(ctx, x, x, x, sh_all, sc_all, *consts)


def _scan_chunk(r, kap, v, kd, ak, lw, h, rev, bmask, bmask_f, tri):
    c = CHUNK
    t_idx = lax.broadcasted_iota(jnp.int32, (c, DR), 0)
    s_idx = lax.broadcasted_iota(jnp.int32, (c, DR), 1) % c
    if rev:
        strict = s_idx > t_idx
        incl = s_idx >= t_idx
    else:
        strict = s_idx < t_idx
        incl = s_idx <= t_idx
    eye = (s_idx == t_idx).astype(F32)

    cum = jnp.dot(tri, lw, precision=HI, preferred_element_type=F32)
    tot = cum[0:1] if rev else cum[c - 1:c]
    e_in = jnp.exp(cum)
    e_out = jnp.exp(-cum)
    e_rest = jnp.exp(tot - cum)
    kt = kap * jnp.exp(cum - lw)
    bh = ak * e_out
    kh = kd * e_out
    rt = r * e_in
    kbar = kd * e_rest
    bbar = ak * e_rest

    def bd(z):
        return jnp.concatenate([z.astype(BF16)] * NHEAD, axis=0) * bmask

    def unbd(full):
        return (full * bmask_f).reshape(NHEAD, HEAD, DR).sum(axis=0)

    def mm(lhs, rhs):
        return jnp.dot(lhs.astype(BF16), rhs, preferred_element_type=F32)

    amat = lax.dot_general(jnp.concatenate([kt, rt], axis=0).astype(BF16),
                           jnp.concatenate([bd(bh), bd(kh)], axis=0),
                           (((1,), (1,)), ((), ())), preferred_element_type=F32)
    n0 = jnp.where(strict, -amat[:c, :DR], 0.0)
    akk = jnp.where(strict, amat[:c, DR:], 0.0)
    arb = jnp.where(incl, amat[c:, :DR], 0.0)
    ark = jnp.where(incl, amat[c:, DR:], 0.0)

    p = n0
    tinv = eye + n0
    p = mm(p, bd(p))
    for _ in range(int(math.log2(c)) - 2):
        out = mm(jnp.concatenate([p, tinv], axis=0), bd(p))
        tinv = tinv + out[c:]
        p = out[:c]
    tinv = tinv + mm(tinv, bd(p))

    bdv = bd(v)
    av = mm(jnp.concatenate([akk, ark], axis=0), bdv)
    wu = mm(tinv, jnp.concatenate([bd(kt), bd(av[:c])], axis=1))
    w = wu[:, :DR]
    uloc = wu[:, DR:]

    lhs_t = jnp.concatenate([kbar, -bbar], axis=0).astype(BF16)
    rhs_t = jnp.concatenate([jnp.concatenate([v, uloc], axis=0),
                             jnp.concatenate([jnp.zeros_like(w), w], axis=0)], axis=1).astype(BF16)
    full = lax.dot_general(lhs_t, rhs_t, (((0,), (0,)), ((), ())), preferred_element_type=F32)
    gcat = unbd(full[:, :DR])
    corr = unbd(full[:, DR:])

    bwu = mm(arb, jnp.concatenate([bd(w), bd(uloc)], axis=1))
    rprime = rt - bwu[:, :DR]
    yloc = av[c:] - bwu[:, DR:]

    out = mm(jnp.concatenate([rprime, corr], axis=0), bd(h))
    pct = jnp.exp(jnp.broadcast_to(tot, (128, DR))).T
    low = lax.broadcasted_iota(jnp.int32, (HEAD, 128), 1) < HEAD
    pcb = jnp.concatenate([jnp.where(low, pct[2 * q * HEAD:(2 * q + 1) * HEAD], pct[(2 * q + 1) * HEAD:(2 * q + 2) * HEAD])
                           for q in range(NHEAD // 2)], axis=1)
    h_new = pcb * h + out[c:] + gcat
    return yloc + out[:c], h_new


def _scan_kernel(rf, kapf, vf, kdf, akf, lwf, rb, kapb, vb, kdb, akb, lwb, bm_ref, bmf_ref, trif_ref, trib_ref,
                 yf_o, yb_o, hf_s, hb_s):
    @pl.when(pl.program_id(1) == 0)
    def _():
        hf_s[...] = jnp.zeros_like(hf_s)
        hb_s[...] = jnp.zeros_like(hb_s)

    bm = bm_ref[...]
    bmf = bmf_ref[...]
    chains = []
    for i in range(rf.shape[0]):
        chains.append((yf_o, hf_s, i, (rf[i], kapf[i], vf[i], kdf[i], akf[i], lwf[i], hf_s[i], False, bm, bmf, trif_ref[...])))
        chains.append((yb_o, hb_s, i, (rb[i], kapb[i], vb[i], kdb[i], akb[i], lwb[i], hb_s[i], True, bm, bmf, trib_ref[...])))
    results = [_scan_chunk(*args) for (_, _, _, args) in chains]
    for (y_o, h_s, i, _), (y, h_new) in zip(chains, results):
        y_o[i] = y
        h_s[i] = h_new


def _scan_call(r, kap, v, kdf, kdb, akf, akb, lwf, lwb, n_ctx):
    b, tt, _ = r.shape
    nch = tt // CHUNK
    ncc = n_ctx // CHUNK
    hidx = jnp.arange(DR) // HEAD
    bmask_f = (hidx[:, None] == hidx[None, :]).astype(F32)
    ii = jnp.arange(CHUNK)
    tri_f = (ii[None, :] <= ii[:, None]).astype(F32)
    tri_b = (ii[None, :] >= ii[:, None]).astype(F32)

    def bwd_chunk(s):
        return jnp.where(s < ncc, ncc - 1 - s, nch - 1 - (s - ncc))

    nb = SCAN_NB if b % SCAN_NB == 0 else 1
    fspec = pl.BlockSpec((nb, CHUNK, DR), lambda bb, s: (bb, s, 0))
    bspec = pl.BlockSpec((nb, CHUNK, DR), lambda bb, s: (bb, bwd_chunk(s), 0))
    consts = (bmask_f.astype(BF16), bmask_f, tri_f, tri_b)
    sds = jax.ShapeDtypeStruct((b, tt, DR), F32)
    return pl.pallas_call(
        _scan_kernel,
        grid=(b // nb, nch),
        in_specs=[fspec] * 6 + [bspec] * 6 + [_const_spec(z.shape) for z in consts],
        out_specs=[fspec, bspec],
        out_shape=[sds, sds],
        scratch_shapes=[pltpu.VMEM((nb, HEAD, DR), F32), pltpu.VMEM((nb, HEAD, DR), F32)],
        compiler_params=_cparams(("arbitrary", "arbitrary")),
        name="scan",
    )(r, kap, v, kdf, akf, lwf, r, kap, v, kdb, akb, lwb, *consts)


def _post_kernel(yf_ref, yb_ref, bon_ref, gate_ref, yc_ref, x_ref, gt1_ref, sh2_ref, sc2_ref, g1_ref, g2_ref,
                 lg_ref, lb_ref, wout_ref, wq_ref, sk_ref, gmean_ref, x1_o, hx_o, st_o):
    gmean = gmean_ref[...]

    def group_mean(z):
        return _dot_ones(z, gmean) * (1.0 / HEAD)

    y = yf_ref[0] + yb_ref[0]
    dlt = y - group_mean(y)
    yn = dlt * lax.rsqrt(group_mean(dlt * dlt) + LNX_EPS) * lg_ref[...] + lb_ref[...]
    y_rwkv = (yn + bon_ref[0]) * gate_ref[0]
    cat = jnp.concatenate([yc_ref[0], y_rwkv], axis=-1).astype(BF16)
    o = jnp.dot(cat, wout_ref[...], preferred_element_type=F32)
    x1 = x_ref[0] + gt1_ref[0] * (_rms(o) * g1_ref[...])
    x1_o[0] = x1
    hx = _rms(x1) * g2_ref[...] * (1.0 + sc2_ref[0]) + sh2_ref[0]
    hx_o[0] = hx
    q = jnp.dot(hx.astype(BF16), wq_ref[...], preferred_element_type=F32).astype(BF16)
    for l in range(2 * PEER_HEADS):
        ql = q[:, l * 128:(l + 1) * 128]
        st_o[l] = lax.dot_general(sk_ref[l], ql, (((1,), (1,)), ((), ())), preferred_element_type=F32)


def _post_call(yf, yb, bonus, gate, yconv, x, gt1, sh2, sc2, g1, g2, lnx_g, lnx_b, wout, wq, sk, gmean, n_ctx):
    b, t, d = x.shape
    nt = t // TM
    off = n_ctx // TM
    mix_spec = pl.BlockSpec((1, TM, DR), lambda bb, i: (bb, i + off, 0))
    x_spec = pl.BlockSpec((1, TM, d), lambda bb, i: (bb, i, 0))
    mod_spec = pl.BlockSpec((1, 1, d), lambda bb, i: (bb, 0, 0))
    consts = (g1, g2, lnx_g, lnx_b, wout, wq, sk, gmean)
    nl = 2 * PEER_HEADS
    return pl.pallas_call(
        _post_kernel,
        grid=(b, nt),
        in_specs=[mix_spec] * 5 + [x_spec] + [mod_spec] * 3 + [_const_spec(z.shape) for z in consts],
        out_specs=[x_spec, x_spec, pl.BlockSpec((nl, NKEYS, TM), lambda bb, i: (0, 0, bb * nt + i))],
        out_shape=[jax.ShapeDtypeStruct((b, t, d), F32), jax.ShapeDtypeStruct((b, t, d), F32),
                   jax.ShapeDtypeStruct((nl, NKEYS, b * t), F32)],
        compiler_params=_cparams(("arbitrary", "arbitrary")),
        name="post",
    )(yf, yb, bonus, gate, yconv, x, gt1, sh2, sc2, *consts)


def _top16(xv, order):
    big = jnp.int32(1 << 30)
    vals, picks, hits = [], [], []
    for _ in range(TOPK):
        m = jnp.max(xv, axis=0, keepdims=True)
        am = jnp.min(jnp.where(xv == m, order, big), axis=0, keepdims=True)
        hit = order == am
        vals.append(m)
        picks.append(am)
        hits.append(hit)
        xv = jnp.where(hit, -jnp.inf, xv)
    return vals, picks, hits


def _topk_kernel(st_ref, ids_o, g_o):
    tk = st_ref.shape[-1]
    rowi = lax.broadcasted_iota(jnp.int32, (NKEYS, tk), 0)
    crow = lax.broadcasted_iota(jnp.int32, (10 * 8, tk), 0)
    ca = jnp.where(crow < 16, 0, jnp.where(crow < 72, 1 + (crow - 16) // 8, crow - 64))
    cb = jnp.where(crow < 16, crow, jnp.where(crow < 72, (crow - 16) % 8, 0))
    stair = (ca + 1) * (cb + 1) <= TOPK
    order2 = ca * TOPK + cb
    ids_all, g_all = [], []
    for h in range(PEER_HEADS):
        tops = []
        for half in range(2):
            vals, picks, _ = _top16(st_ref[2 * h + half], rowi)
            tops.append((jnp.concatenate(vals, axis=0), jnp.concatenate(picks, axis=0)))
        (vi, ii), (vj, ij) = tops
        cand = jnp.concatenate([vi[0:1] + vj] + [vi[a:a + 1] + vj[0:8] for a in range(1, 8)] + [vi[8:16] + vj[0:1]],
                               axis=0)
        eid = jnp.concatenate([ii[0:1] * NKEYS + ij] + [ii[a:a + 1] * NKEYS + ij[0:8] for a in range(1, 8)]
                              + [ii[8:16] * NKEYS + ij[0:1]], axis=0)
        cand = jnp.where(stair, cand, -jnp.inf)
        vals, _, hits = _top16(cand, order2)
        sc = jnp.concatenate(vals, axis=0)
        ex = jnp.concatenate([jnp.max(jnp.where(s, eid, -1), axis=0, keepdims=True) for s in hits], axis=0)
        e = jnp.exp(sc - sc[0:1])
        g_all.append(e / jnp.sum(e, axis=0, keepdims=True))
        ids_all.append(ex)
    ids_o[...] = jnp.concatenate(ids_all, axis=0).T
    g_o[...] = jnp.concatenate(g_all, axis=0).T


def _topk_call(st):
    nl, nk, n = st.shape
    return pl.pallas_call(
        _topk_kernel,
        grid=(n // TK,),
        in_specs=[pl.BlockSpec((nl, nk, TK), lambda i: (0, 0, i))],
        out_specs=[pl.BlockSpec((TK, NPAIR), lambda i: (i, 0))] * 2,
        out_shape=[jax.ShapeDtypeStruct((n, NPAIR), jnp.int32), jax.ShapeDtypeStruct((n, NPAIR), F32)],
        compiler_params=_cparams(("arbitrary",)),
        name="topk",
    )(st)


def _gelu(x):
    return 0.5 * x * (1.0 + lax.erf(x * (1.0 / math.sqrt(2.0))))


NSLOT = 4


def _peer_kernel(ids_ref, g_ref, hx_ref, x1_ref, gt2_ref, g3_ref, uv_hbm, o_ref, *scratch):
    bufs, sem, acc = scratch[:NSLOT], scratch[NSLOT], scratch[NSLOT + 1]
    tk, d = hx_ref.shape
    nc = d // 128

    def issue(t, slot):
        for p in range(NPAIR):
            e = ids_ref[t * NPAIR + p]
            pltpu.make_async_copy(uv_hbm.at[e], bufs[slot].at[:, p, :], sem.at[slot]).start(priority=p % 2)

    def wait(slot):
        pltpu.make_async_copy(bufs[slot], bufs[slot], sem.at[slot]).wait()

    rowi = lax.broadcasted_iota(jnp.int32, (8, NPAIR), 0)

    def group(base, static_tail):
        x8 = hx_ref[pl.ds(base, 8), :].astype(BF16)
        g8 = g_ref[pl.ds(base, 8), :]
        o8 = jnp.zeros((8, d), F32)
        for j in range(8):
            slot = j % NSLOT
            wait(slot)
            u = jnp.concatenate([bufs[slot][s] for s in range(nc)], axis=1).astype(BF16)
            vv = jnp.concatenate([bufs[slot][nc + s] for s in range(nc)], axis=1).astype(BF16)
            act = lax.dot_general(x8, u, (((1,), (1,)), ((), ())), preferred_element_type=F32)
            w = jnp.where(rowi == j, g8 * _gelu(act), 0.0)
            o8 = o8 + jnp.dot(w.astype(BF16), vv, preferred_element_type=F32)
            if not static_tail or j + NSLOT < 8:
                issue(base + j + NSLOT, slot)
        acc[pl.ds(base, 8), :] = o8

    for t in range(NSLOT):
        issue(t, t)

    def body(gi, carry):
        group(pl.multiple_of(gi * 8, 8), False)
        return carry

    lax.fori_loop(0, tk // 8 - 1, body, 0)
    group(tk - 8, True)
    o_ref[...] = x1_ref[...] + gt2_ref[0] * (_rms(acc[...]) * g3_ref[...])


def _peer_call(ids, gates, hx, x1, gt2, g3, uv, tokens_per_batch):
    n, d = hx.shape
    per_b = tokens_per_batch // TK
    tok_spec = pl.BlockSpec((TK, d), lambda i: (i, 0))
    return pl.pallas_call(
        _peer_kernel,
        grid=(n // TK,),
        in_specs=[pl.BlockSpec((TK * NPAIR,), lambda i: (i,), memory_space=pltpu.SMEM),
                  pl.BlockSpec((TK, NPAIR), lambda i: (i, 0)),
                  tok_spec, tok_spec,
                  pl.BlockSpec((1, 1, d), lambda i: (i // per_b, 0, 0)),
                  _const_spec(g3.shape),
                  pl.BlockSpec(memory_space=pl.ANY)],
        out_specs=tok_spec,
        out_shape=jax.ShapeDtypeStruct((n, d), F32),
        scratch_shapes=[pltpu.VMEM((2 * d // 128, NPAIR, 128), F32) for _ in range(NSLOT)]
        + [pltpu.SemaphoreType.DMA((NSLOT,)), pltpu.VMEM((TK, d), F32)],
        compiler_params=_cparams(("arbitrary",)),
        name="peer",
    )(ids, gates, hx, x1, gt2, g3, uv)


def kernel(x, c, ctx, c_ctx, w_mod, b_mod, norm_gains, w_in, w_out, conv_w, tshift_mu, decay_w0, decay_w2, iclr_a0,
           iclr_a2, gate_w2, k_k, k_a, r_k, lnx_g, lnx_b, peer_wq, peer_subkeys, peer_u, peer_v):
    b, t, d = x.shape
    n_ctx = ctx.shape[1]
    assert w_mod.shape[0] == 1 and d == 2 * DR and t % TM == 0 and n_ctx == TM and b + 1 <= 8
    ng = norm_gains[0]

    cin = jnp.concatenate([c, c_ctx[None], jnp.zeros((8 - b - 1, d), F32)], axis=0)
    mod = _mod_call(cin, w_mod[0], b_mod[0][None])
    lat = mod[:b].reshape(b, 6, d)
    cm = mod[b].reshape(6, d)

    def both(j):
        return jnp.stack([jnp.broadcast_to(cm[j], (b, d)), lat[:, j]], axis=1)[:, :, None, :]

    zeros = jnp.zeros((HEAD, DR), F32)
    w2 = jnp.concatenate([jnp.concatenate([decay_w2[0, 0], zeros], axis=1),
                          jnp.concatenate([zeros, decay_w2[0, 1]], axis=1)], axis=0)
    a2 = jnp.concatenate([jnp.concatenate([iclr_a2[0, 0], zeros], axis=1),
                          jnp.concatenate([zeros, iclr_a2[0, 1]], axis=1)], axis=0)
    hidx = jnp.arange(DR) // HEAD
    gsum = (hidx[:, None] == hidx[None, :]).astype(BF16)

    feats = _prep_call(ctx, x, both(0), both(1), ng[0][None], w_in[0].astype(BF16), tshift_mu[0].reshape(6, DR),
                       k_k[0][None], k_a[0][None], r_k[0].reshape(1, DR), decay_w0[0].reshape(1, 2 * DR), w2,
                       iclr_a0[0].reshape(1, 2 * DR), a2.astype(BF16), gate_w2[0].astype(BF16), conv_w[0], gsum)
    r, kap, v, kdf, kdb, akf, akb, lwf, lwb, yconv, gate, bonus = feats

    yf, yb = _scan_call(r, kap, v, kdf, kdb, akf, akb, lwf, lwb, n_ctx)

    sk = peer_subkeys[0].reshape(2 * PEER_HEADS, NKEYS, -1).astype(BF16)
    x1, hx, st = _post_call(yf, yb, bonus, gate, yconv, x, lat[:, 2][:, None], lat[:, 3][:, None], lat[:, 4][:, None],
                            ng[1][None], ng[2][None], lnx_g[0][None], lnx_b[0][None], w_out[0].astype(BF16),
                            peer_wq[0].astype(BF16), sk, gsum, n_ctx)

    ids, gates = _topk_call(st)
    uv = jnp.concatenate([peer_u[0], peer_v[0]], axis=1).reshape(-1, 2 * d // 128, 128)
    out = _peer_call(ids.reshape(-1), gates, hx.reshape(b * t, d), x1.reshape(b * t, d), lat[:, 5][:, None], ng[3][None], uv, t)
    return out.reshape(b, t, d)
```

```python
import functools
import math

import jax
import jax.numpy as jnp
from jax import lax
from jax.experimental import pallas as pl
from jax.experimental.pallas import tpu as pltpu

F32 = jnp.float32
BF16 = jnp.bfloat16
HI = lax.Precision.HIGHEST

NORM_EPS = 1e-6
LNX_EPS = 64e-5
HEAD = 64
NHEAD = 8
DR = HEAD * NHEAD
GRID_W = 64
CHUNK = 64
TOPK = 16
NKEYS = 128
PEER_HEADS = 8
NPAIR = PEER_HEADS * TOPK
TM = 256
TK = 128
SCAN_NB = 2
VMEM_LIMIT = 56 * 1024 * 1024


def _cparams(sem):
    return pltpu.CompilerParams(dimension_semantics=sem, vmem_limit_bytes=VMEM_LIMIT)


def _const_spec(shape):
    nd = len(shape)
    return pl.BlockSpec(shape, lambda *_: (0,) * nd)


def _rms(x):
    return x * lax.rsqrt(jnp.mean(x * x, axis=-1, keepdims=True) + NORM_EPS)


def _dot_ones(z, ones):
    hi = z.astype(BF16)
    lo = (z - hi.astype(F32)).astype(BF16)
    both = jnp.dot(jnp.concatenate([hi, lo], axis=0), ones, preferred_element_type=F32)
    return both[:z.shape[0]] + both[z.shape[0]:]


def _mod_kernel(c_ref, w_ref, b_ref, o_ref):
    c = c_ref[...]
    s = c * jax.nn.sigmoid(c)
    o_ref[...] = jnp.dot(s.astype(BF16), w_ref[...].astype(BF16), preferred_element_type=F32) + b_ref[...]


def _mod_call(cin, w, b):
    rows, d = cin.shape
    n = w.shape[1]
    tn = 1024
    return pl.pallas_call(
        _mod_kernel,
        grid=(n // tn,),
        in_specs=[_const_spec((rows, d)), pl.BlockSpec((d, tn), lambda j: (0, j)), pl.BlockSpec((1, tn), lambda j: (0, j))],
        out_specs=pl.BlockSpec((rows, tn), lambda j: (0, j)),
        out_shape=jax.ShapeDtypeStruct((rows, n), F32),
        compiler_params=_cparams(("arbitrary",)),
        name="mod",
    )(cin, w, b)


def _prep_kernel(nt, ctx_ref, xm_ref, xp_ref, xn_ref, sh_ref, sc_ref, g0_ref, win_ref, mu_ref, kk_ref, ka_ref, rk_ref,
                 w0_ref, w2_ref, a0_ref, a2_ref, gw2_ref, cw_ref, gsum_ref,
                 r_o, kap_o, v_o, kdf_o, kdb_o, akf_o, akb_o, lwf_o, lwb_o, yc_o, gate_o, bon_o):
    i = pl.program_id(1)
    g0 = g0_ref[...]
    sh = sh_ref[0, 0]
    sc = sc_ref[0, 0]

    def norm_mod(x):
        return (_rms(x) * g0 * (1.0 + sc) + sh).astype(BF16)

    hb = norm_mod(jnp.where(i == 0, ctx_ref[0], xm_ref[0]))
    halo = norm_mod(jnp.concatenate([xp_ref[0], xn_ref[0]], axis=0))

    p_conv = jnp.dot(hb, win_ref[:, 0:3 * DR], preferred_element_type=F32)
    p_rkv = jnp.dot(hb, win_ref[:, 3 * DR:6 * DR], preferred_element_type=F32)
    p_lo = jnp.dot(hb, win_ref[:, 6 * DR:], preferred_element_type=F32)
    p_halo = jnp.dot(halo, win_ref[:, 3 * DR:6 * DR], preferred_element_type=F32)

    has_prev = (i >= 2).astype(F32)
    has_next = jnp.logical_and(i >= 1, i <= nt - 2).astype(F32)
    first = p_halo[7:8] * has_prev
    last = p_halo[8:9] * has_next

    rows = lax.broadcasted_iota(jnp.int32, (TM, 1), 0)

    def prev_of(z, row0):
        return jnp.where(rows == 0, row0, pltpu.roll(z, 1, axis=0))

    def next_of(z, rowl):
        return jnp.where(rows == TM - 1, rowl, pltpu.roll(z, TM - 1, axis=0))

    mu = mu_ref[...]

    def tshift(j):
        z = p_rkv[:, j * DR:(j + 1) * DR]
        zp = prev_of(z, first[:, j * DR:(j + 1) * DR])
        zn = next_of(z, last[:, j * DR:(j + 1) * DR])
        return z + mu[j:j + 1] * (zp - z) + mu[3 + j:4 + j] * (zn - z)

    r = tshift(0)
    k = tshift(1)
    v = tshift(2)

    gsum = gsum_ref[...]

    kk = k * kk_ref[...]
    sums = _dot_ones(jnp.concatenate([kk * kk, r * k * rk_ref[...]], axis=0), gsum)
    kap = kk * lax.rsqrt(sums[:TM] + 1e-12)
    bonus = sums[TM:] * v

    lo_w = jnp.tanh(p_lo[:, 128:256])
    zw = jnp.dot(lo_w, w2_ref[...], precision=HI, preferred_element_type=F32) + w0_ref[...]
    lw = -math.exp(-0.5) * jax.nn.sigmoid(zw)
    za = jnp.dot(p_lo[:, 256:384].astype(BF16), a2_ref[...], preferred_element_type=F32) + a0_ref[...]
    a = jax.nn.sigmoid(za)
    ka = ka_ref[...]

    r_o[0] = r
    kap_o[0] = kap
    v_o[0] = v
    for d, (kd_o, ak_o, lw_o) in enumerate(((kdf_o, akf_o, lwf_o), (kdb_o, akb_o, lwb_o))):
        a_d = a[:, d * DR:(d + 1) * DR]
        kd_o[0] = k * (1.0 + (a_d - 1.0) * ka)
        ak_o[0] = a_d * kap
        lw_o[0] = lw[:, d * DR:(d + 1) * DR]

    gate_o[0] = jnp.dot(jax.nn.sigmoid(p_lo[:, 0:128]).astype(BF16), gw2_ref[...], preferred_element_type=F32)
    bon_o[0] = bonus

    z = p_conv[:, DR:2 * DR] * p_conv[:, 2 * DR:3 * DR]
    col = rows % GRID_W
    zp = jnp.where(col == 0, 0.0, pltpu.roll(z, 1, axis=0))
    zn = jnp.where(col == GRID_W - 1, 0.0, pltpu.roll(z, TM - 1, axis=0))
    cw = cw_ref[...]
    yc_o[0] = p_conv[:, 0:DR] * (cw[0:1] * zp + cw[1:2] * z + cw[2:3] * zn)


def _prep_call(ctx, x, sh_all, sc_all, g0, win, mu6, k_k, k_a, r_k, w0, w2, a0, a2, gw2, cw, gsum):
    b, t, d = x.shape
    tt = t + ctx.shape[1]
    nt = tt // TM
    nb8 = t // 8
    per8 = TM // 8
    ctx_spec = pl.BlockSpec((1, TM, d), lambda bb, i: (bb, 0, 0))
    row_spec = pl.BlockSpec((1, TM, d), lambda bb, i: (bb, jnp.maximum(i - 1, 0), 0))
    prev_spec = pl.BlockSpec((1, 8, d), lambda bb, i: (bb, jnp.maximum((i - 1) * per8 - 1, 0), 0))
    next_spec = pl.BlockSpec((1, 8, d), lambda bb, i: (bb, jnp.clip(i * per8, 0, nb8 - 1), 0))
    mod_spec = pl.BlockSpec((1, 1, 1, d), lambda bb, i: (bb, jnp.minimum(i, 1), 0, 0))
    consts = (g0, win, mu6, k_k, k_a, r_k, w0, w2, a0, a2, gw2, cw, gsum)
    out_spec = pl.BlockSpec((1, TM, DR), lambda bb, i: (bb, i, 0))
    out_sds = jax.ShapeDtypeStruct((b, tt, DR), F32)
    return pl.pallas_call(
        functools.partial(_prep_kernel, nt),
        grid=(b, nt),
        in_specs=[ctx_spec, row_spec, prev_spec, next_spec, mod_spec, mod_spec] + [_const_spec(z.shape) for z in consts],
        out_specs=[out_spec] * 12,
        out_shape=[out_sds] * 12,
        compiler_params=_cparams(("arbitrary", "arbitrary")),
        name="prep",
    )(ctx, x, x, x, sh_all, sc_all, *consts)


def _scan_chunk(r, kap, v, kd, ak, lw, h, rev, bmask, bmask_f, tri):
    c = CHUNK
    t_idx = lax.broadcasted_iota(jnp.int32, (c, DR), 0)
    s_idx = lax.broadcasted_iota(jnp.int32, (c, DR), 1) % c
    if rev:
        strict = s_idx > t_idx
        incl = s_idx >= t_idx
    else:
        strict = s_idx < t_idx
        incl = s_idx <= t_idx
    eye = (s_idx == t_idx).astype(F32)

    cum = jnp.dot(tri, lw, precision=HI, preferred_element_type=F32)
    tot = cum[0:1] if rev else cum[c - 1:c]
    e_in = jnp.exp(cum)
    e_out = jnp.exp(-cum)
    e_rest = jnp.exp(tot - cum)
    kt = kap * jnp.exp(cum - lw)
    bh = ak * e_out
    kh = kd * e_out
    rt = r * e_in
    kbar = kd * e_rest
    bbar = ak * e_rest

    def bd(z):
        return jnp.concatenate([z.astype(BF16)] * NHEAD, axis=0) * bmask

    def unbd(full):
        return (full * bmask_f).reshape(NHEAD, HEAD, DR).sum(axis=0)

    def mm(lhs, rhs):
        return jnp.dot(lhs.astype(BF16), rhs, preferred_element_type=F32)

    amat = lax.dot_general(jnp.concatenate([kt, rt], axis=0).astype(BF16),
                           jnp.concatenate([bd(bh), bd(kh)], axis=0),
                           (((1,), (1,)), ((), ())), preferred_element_type=F32)
    n0 = jnp.where(strict, -amat[:c, :DR], 0.0)
    akk = jnp.where(strict, amat[:c, DR:], 0.0)
    arb = jnp.where(incl, amat[c:, :DR], 0.0)
    ark = jnp.where(incl, amat[c:, DR:], 0.0)

    p = n0
    tinv = eye + n0
    p = mm(p, bd(p))
    for _ in range(int(math.log2(c)) - 2):
        out = mm(jnp.concatenate([p, tinv], axis=0), bd(p))
        tinv = tinv + out[c:]
        p = out[:c]
    tinv = tinv + mm(tinv, bd(p))

    bdv = bd(v)
    av = mm(jnp.concatenate([akk, ark], axis=0), bdv)
    wu = mm(tinv, jnp.concatenate([bd(kt), bd(av[:c])], axis=1))
    w = wu[:, :DR]
    uloc = wu[:, DR:]

    lhs_t = jnp.concatenate([kbar, -bbar], axis=0).astype(BF16)
    rhs_t = jnp.concatenate([jnp.concatenate([v, uloc], axis=0),
                             jnp.concatenate([jnp.zeros_like(w), w], axis=0)], axis=1).astype(BF16)
    full = lax.dot_general(lhs_t, rhs_t, (((0,), (0,)), ((), ())), preferred_element_type=F32)
    gcat = unbd(full[:, :DR])
    corr = unbd(full[:, DR:])

    bwu = mm(arb, jnp.concatenate([bd(w), bd(uloc)], axis=1))
    rprime = rt - bwu[:, :DR]
    yloc = av[c:] - bwu[:, DR:]

    out = mm(jnp.concatenate([rprime, corr], axis=0), bd(h))
    pct = jnp.exp(jnp.broadcast_to(tot, (128, DR))).T
    low = lax.broadcasted_iota(jnp.int32, (HEAD, 128), 1) < HEAD
    pcb = jnp.concatenate([jnp.where(low, pct[2 * q * HEAD:(2 * q + 1) * HEAD], pct[(2 * q + 1) * HEAD:(2 * q + 2) * HEAD])
                           for q in range(NHEAD // 2)], axis=1)
    h_new = pcb * h + out[c:] + gcat
    return yloc + out[:c], h_new


def _scan_kernel(rf, kapf, vf, kdf, akf, lwf, rb, kapb, vb, kdb, akb, lwb, bm_ref, bmf_ref, trif_ref, trib_ref,
                 yf_o, yb_o, hf_s, hb_s):
    @pl.when(pl.program_id(1) == 0)
    def _():
        hf_s[...] = jnp.zeros_like(hf_s)
        hb_s[...] = jnp.zeros_like(hb_s)

    bm = bm_ref[...]
    bmf = bmf_ref[...]
    chains = []
    for i in range(rf.shape[0]):
        chains.append((yf_o, hf_s, i, (rf[i], kapf[i], vf[i], kdf[i], akf[i], lwf[i], hf_s[i], False, bm, bmf, trif_ref[...])))
        chains.append((yb_o, hb_s, i, (rb[i], kapb[i], vb[i], kdb[i], akb[i], lwb[i], hb_s[i], True, bm, bmf, trib_ref[...])))
    results = [_scan_chunk(*args) for (_, _, _, args) in chains]
    for (y_o, h_s, i, _), (y, h_new) in zip(chains, results):
        y_o[i] = y
        h_s[i] = h_new


def _scan_call(r, kap, v, kdf, kdb, akf, akb, lwf, lwb, n_ctx):
    b, tt, _ = r.shape
    nch = tt // CHUNK
    ncc = n_ctx // CHUNK
    hidx = jnp.arange(DR) // HEAD
    bmask_f = (hidx[:, None] == hidx[None, :]).astype(F32)
    ii = jnp.arange(CHUNK)
    tri_f = (ii[None, :] <= ii[:, None]).astype(F32)
    tri_b = (ii[None, :] >= ii[:, None]).astype(F32)

    def bwd_chunk(s):
        return jnp.where(s < ncc, ncc - 1 - s, nch - 1 - (s - ncc))

    nb = SCAN_NB if b % SCAN_NB == 0 else 1
    fspec = pl.BlockSpec((nb, CHUNK, DR), lambda bb, s: (bb, s, 0))
    bspec = pl.BlockSpec((nb, CHUNK, DR), lambda bb, s: (bb, bwd_chunk(s), 0))
    consts = (bmask_f.astype(BF16), bmask_f, tri_f, tri_b)
    sds = jax.ShapeDtypeStruct((b, tt, DR), F32)
    return pl.pallas_call(
        _scan_kernel,
        grid=(b // nb, nch),
        in_specs=[fspec] * 6 + [bspec] * 6 + [_const_spec(z.shape) for z in consts],
        out_specs=[fspec, bspec],
        out_shape=[sds, sds],
        scratch_shapes=[pltpu.VMEM((nb, HEAD, DR), F32), pltpu.VMEM((nb, HEAD, DR), F32)],
        compiler_params=_cparams(("arbitrary", "arbitrary")),
        name="scan",
    )(r, kap, v, kdf, akf, lwf, r, kap, v, kdb, akb, lwb, *consts)


def _post_kernel(yf_ref, yb_ref, bon_ref, gate_ref, yc_ref, x_ref, gt1_ref, sh2_ref, sc2_ref, g1_ref, g2_ref,
                 lg_ref, lb_ref, wout_ref, wq_ref, sk_ref, gmean_ref, x1_o, hx_o, st_o):
    gmean = gmean_ref[...]

    def group_mean(z):
        return _dot_ones(z, gmean) * (1.0 / HEAD)

    y = yf_ref[0] + yb_ref[0]
    dlt = y - group_mean(y)
    yn = dlt * lax.rsqrt(group_mean(dlt * dlt) + LNX_EPS) * lg_ref[...] + lb_ref[...]
    y_rwkv = (yn + bon_ref[0]) * gate_ref[0]
    cat = jnp.concatenate([yc_ref[0], y_rwkv], axis=-1).astype(BF16)
    o = jnp.dot(cat, wout_ref[...], preferred_element_type=F32)
    x1 = x_ref[0] + gt1_ref[0] * (_rms(o) * g1_ref[...])
    x1_o[0] = x1
    hx = _rms(x1) * g2_ref[...] * (1.0 + sc2_ref[0]) + sh2_ref[0]
    hx_o[0] = hx
    q = jnp.dot(hx.astype(BF16), wq_ref[...], preferred_element_type=F32).astype(BF16)
    for l in range(2 * PEER_HEADS):
        ql = q[:, l * 128:(l + 1) * 128]
        st_o[l] = lax.dot_general(sk_ref[l], ql, (((1,), (1,)), ((), ())), preferred_element_type=F32)


def _post_call(yf, yb, bonus, gate, yconv, x, gt1, sh2, sc2, g1, g2, lnx_g, lnx_b, wout, wq, sk, gmean, n_ctx):
    b, t, d = x.shape
    nt = t // TM
    off = n_ctx // TM
    mix_spec = pl.BlockSpec((1, TM, DR), lambda bb, i: (bb, i + off, 0))
    x_spec = pl.BlockSpec((1, TM, d), lambda bb, i: (bb, i, 0))
    mod_spec = pl.BlockSpec((1, 1, d), lambda bb, i: (bb, 0, 0))
    consts = (g1, g2, lnx_g, lnx_b, wout, wq, sk, gmean)
    nl = 2 * PEER_HEADS
    return pl.pallas_call(
        _post_kernel,
        grid=(b, nt),
        in_specs=[mix_spec] * 5 + [x_spec] + [mod_spec] * 3 + [_const_spec(z.shape) for z in consts],
        out_specs=[x_spec, x_spec, pl.BlockSpec((nl, NKEYS, TM), lambda bb, i: (0, 0, bb * nt + i))],
        out_shape=[jax.ShapeDtypeStruct((b, t, d), F32), jax.ShapeDtypeStruct((b, t, d), F32),
                   jax.ShapeDtypeStruct((nl, NKEYS, b * t), F32)],
        compiler_params=_cparams(("arbitrary", "arbitrary")),
        name="post",
    )(yf, yb, bonus, gate, yconv, x, gt1, sh2, sc2, *consts)


def _top16(xv, order):
    big = jnp.int32(1 << 30)
    vals, picks, hits = [], [], []
    for _ in range(TOPK):
        m = jnp.max(xv, axis=0, keepdims=True)
        am = jnp.min(jnp.where(xv == m, order, big), axis=0, keepdims=True)
        hit = order == am
        vals.append(m)
        picks.append(am)
        hits.append(hit)
        xv = jnp.where(hit, -jnp.inf, xv)
    return vals, picks, hits


def _topk_head(s_i, s_j):
    tk = s_i.shape[-1]
    rowi = lax.broadcasted_iota(jnp.int32, (NKEYS, tk), 0)
    crow = lax.broadcasted_iota(jnp.int32, (10 * 8, tk), 0)
    ca = jnp.where(crow < 16, 0, jnp.where(crow < 72, 1 + (crow - 16) // 8, crow - 64))
    cb = jnp.where(crow < 16, crow, jnp.where(crow < 72, (crow - 16) % 8, 0))
    stair = (ca + 1) * (cb + 1) <= TOPK
    order2 = ca * TOPK + cb
    tops = []
    for s in (s_i, s_j):
        vals, picks, _ = _top16(s, rowi)
        tops.append((jnp.concatenate(vals, axis=0), jnp.concatenate(picks, axis=0)))
    (vi, ii), (vj, ij) = tops
    cand = jnp.concatenate([vi[0:1] + vj] + [vi[a:a + 1] + vj[0:8] for a in range(1, 8)] + [vi[8:16] + vj[0:1]],
                           axis=0)
    eid = jnp.concatenate([ii[0:1] * NKEYS + ij] + [ii[a:a + 1] * NKEYS + ij[0:8] for a in range(1, 8)]
                          + [ii[8:16] * NKEYS + ij[0:1]], axis=0)
    cand = jnp.where(stair, cand, -jnp.inf)
    vals, _, hits = _top16(cand, order2)
    sc = jnp.concatenate(vals, axis=0)
    ex = jnp.concatenate([jnp.max(jnp.where(s, eid, -1), axis=0, keepdims=True) for s in hits], axis=0)
    e = jnp.exp(sc - sc[0:1])
    return ex, e / jnp.sum(e, axis=0, keepdims=True)


def _gelu(x):
    return 0.5 * x * (1.0 + lax.erf(x * (1.0 / math.sqrt(2.0))))


NSLOT = 4
TBLK = 2 * 8


def _peer_kernel(st0_ref, stn_ref, hx_ref, x1_ref, gt2_ref, g3_ref, uv_hbm, o_ref, *scratch):
    bufs = scratch[:NSLOT]
    sem, ids_sm, ids_v, g_s, sem_ids, idst, gt, acc = scratch[NSLOT:]
    tk, d = hx_ref.shape
    nc = d // 128
    step = pl.program_id(0)
    cur = step % 2
    nxt = 1 - cur

    def issue(half, t, slot):
        for p in range(NPAIR):
            e = ids_sm[half * tk + t, p]
            pltpu.make_async_copy(uv_hbm.at[e], bufs[slot].at[:, p, :], sem.at[slot]).start(priority=p % 2)

    def wait(slot):
        pltpu.make_async_copy(bufs[slot], bufs[slot], sem.at[slot]).wait()

    def publish(half):
        g_s[half] = gt[...].T
        ids_v[...] = idst[...].T
        cp = pltpu.make_async_copy(ids_v, ids_sm.at[pl.ds(half * tk, tk)], sem_ids)
        cp.start()
        cp.wait()

    def retrieve(st_ref, h):
        ex, g = _topk_head(st_ref[2 * h], st_ref[2 * h + 1])
        idst[pl.ds(h * TOPK, TOPK), :] = ex
        gt[pl.ds(h * TOPK, TOPK), :] = g

    @pl.when(step == 0)
    def _():
        for h in range(PEER_HEADS):
            retrieve(st0_ref, h)
        publish(0)
        for t in range(NSLOT):
            issue(0, t, t)

    rowi = lax.broadcasted_iota(jnp.int32, (8, NPAIR), 0)

    def block(m, last):
        for half8 in range(TBLK // 8):
            base = m * TBLK + half8 * 8
            if not last:
                base = pl.multiple_of(base, 8)
            x8 = hx_ref[pl.ds(base, 8), :].astype(BF16)
            g8 = g_s[cur, pl.ds(base, 8), :]
            o8 = jnp.zeros((8, d), F32)
            for j in range(8):
                slot = j % NSLOT
                wait(slot)
                u = jnp.concatenate([bufs[slot][s] for s in range(nc)], axis=1).astype(BF16)
                vv = jnp.concatenate([bufs[slot][nc + s] for s in range(nc)], axis=1).astype(BF16)
                act = lax.dot_general(x8, u, (((1,), (1,)), ((), ())), preferred_element_type=F32)
                w = jnp.where(rowi == j, g8 * _gelu(act), 0.0)
                o8 = o8 + jnp.dot(w.astype(BF16), vv, preferred_element_type=F32)
                if not last or half8 * 8 + j + NSLOT < TBLK:
                    issue(cur, base + j + NSLOT, slot)
            acc[pl.ds(base, 8), :] = o8
        retrieve(stn_ref, m)

    def body(m, carry):
        block(m, False)
        return carry

    nblk = tk // TBLK
    lax.fori_loop(0, nblk - 1, body, 0)
    block(nblk - 1, True)
    publish(nxt)

    @pl.when(step + 1 < pl.num_programs(0))
    def _():
        for t in range(NSLOT):
            issue(nxt, t, t)

    o_ref[...] = x1_ref[...] + gt2_ref[0] * (_rms(acc[...]) * g3_ref[...])


def _peer_call(st, hx, x1, gt2, g3, uv, tokens_per_batch):
    n, d = hx.shape
    nl, nk, _ = st.shape
    assert nl == 2 * PEER_HEADS and nk == NKEYS and TK // TBLK == PEER_HEADS and NPAIR == 128
    nsteps = n // TK
    per_b = tokens_per_batch // TK
    tok_spec = pl.BlockSpec((TK, d), lambda i: (i, 0))
    return pl.pallas_call(
        _peer_kernel,
        grid=(nsteps,),
        in_specs=[pl.BlockSpec((nl, nk, TK), lambda i: (0, 0, 0)),
                  pl.BlockSpec((nl, nk, TK), lambda i: (0, 0, jnp.minimum(i + 1, nsteps - 1))),
                  tok_spec, tok_spec,
                  pl.BlockSpec((1, 1, d), lambda i: (i // per_b, 0, 0)),
                  _const_spec(g3.shape),
                  pl.BlockSpec(memory_space=pl.ANY)],
        out_specs=tok_spec,
        out_shape=jax.ShapeDtypeStruct((n, d), F32),
        scratch_shapes=[pltpu.VMEM((2 * d // 128, NPAIR, 128), F32) for _ in range(NSLOT)]
        + [pltpu.SemaphoreType.DMA((NSLOT,)),
           pltpu.SMEM((2 * TK, NPAIR), jnp.int32),
           pltpu.VMEM((TK, NPAIR), jnp.int32),
           pltpu.VMEM((2, TK, NPAIR), F32),
           pltpu.SemaphoreType.DMA(()),
           pltpu.VMEM((NPAIR, TK), jnp.int32),
           pltpu.VMEM((NPAIR, TK), F32),
           pltpu.VMEM((TK, d), F32)],
        compiler_params=_cparams(("arbitrary",)),
        name="peer",
    )(st, st, hx, x1, gt2, g3, uv)


def kernel(x, c, ctx, c_ctx, w_mod, b_mod, norm_gains, w_in, w_out, conv_w, tshift_mu, decay_w0, decay_w2, iclr_a0,
           iclr_a2, gate_w2, k_k, k_a, r_k, lnx_g, lnx_b, peer_wq, peer_subkeys, peer_u, peer_v):
    b, t, d = x.shape
    n_ctx = ctx.shape[1]
    assert w_mod.shape[0] == 1 and d == 2 * DR and t % TM == 0 and n_ctx == TM and b + 1 <= 8
    ng = norm_gains[0]

    cin = jnp.concatenate([c, c_ctx[None], jnp.zeros((8 - b - 1, d), F32)], axis=0)
    mod = _mod_call(cin, w_mod[0], b_mod[0][None])
    lat = mod[:b].reshape(b, 6, d)
    cm = mod[b].reshape(6, d)

    def both(j):
        return jnp.stack([jnp.broadcast_to(cm[j], (b, d)), lat[:, j]], axis=1)[:, :, None, :]

    zeros = jnp.zeros((HEAD, DR), F32)
    w2 = jnp.concatenate([jnp.concatenate([decay_w2[0, 0], zeros], axis=1),
                          jnp.concatenate([zeros, decay_w2[0, 1]], axis=1)], axis=0)
    a2 = jnp.concatenate([jnp.concatenate([iclr_a2[0, 0], zeros], axis=1),
                          jnp.concatenate([zeros, iclr_a2[0, 1]], axis=1)], axis=0)
    hidx = jnp.arange(DR) // HEAD
    gsum = (hidx[:, None] == hidx[None, :]).astype(BF16)

    feats = _prep_call(ctx, x, both(0), both(1), ng[0][None], w_in[0].astype(BF16), tshift_mu[0].reshape(6, DR),
                       k_k[0][None], k_a[0][None], r_k[0].reshape(1, DR), decay_w0[0].reshape(1, 2 * DR), w2,
                       iclr_a0[0].reshape(1, 2 * DR), a2.astype(BF16), gate_w2[0].astype(BF16), conv_w[0], gsum)
    r, kap, v, kdf, kdb, akf, akb, lwf, lwb, yconv, gate, bonus = feats

    yf, yb = _scan_call(r, kap, v, kdf, kdb, akf, akb, lwf, lwb, n_ctx)

    sk = peer_subkeys[0].reshape(2 * PEER_HEADS, NKEYS, -1).astype(BF16)
    x1, hx, st = _post_call(yf, yb, bonus, gate, yconv, x, lat[:, 2][:, None], lat[:, 3][:, None], lat[:, 4][:, None],
                            ng[1][None], ng[2][None], lnx_g[0][None], lnx_b[0][None], w_out[0].astype(BF16),
                            peer_wq[0].astype(BF16), sk, gsum, n_ctx)

    uv = jnp.concatenate([peer_u[0], peer_v[0]], axis=1).reshape(-1, 2 * d // 128, 128)
    out = _peer_call(st, hx.reshape(b * t, d), x1.reshape(b * t, d), lat[:, 5][:, None], ng[3][None], uv, t)
    return out.reshape(b, t, d)
```

```python
import functools
import math

import jax
import jax.numpy as jnp
from jax import lax
from jax.experimental import pallas as pl
from jax.experimental.pallas import tpu as pltpu

F32 = jnp.float32
BF16 = jnp.bfloat16
HI = lax.Precision.HIGHEST

NORM_EPS = 1e-6
LNX_EPS = 64e-5
HEAD = 64
NHEAD = 8
DR = HEAD * NHEAD
GRID_W = 64
CHUNK = 64
TOPK = 16
NKEYS = 128
PEER_HEADS = 8
NPAIR = PEER_HEADS * TOPK
TM = 256
TK = 128
SCAN_NB = 2
VMEM_LIMIT = 56 * 1024 * 1024


def _cparams(sem):
    return pltpu.CompilerParams(dimension_semantics=sem, vmem_limit_bytes=VMEM_LIMIT)


def _const_spec(shape):
    nd = len(shape)
    return pl.BlockSpec(shape, lambda *_: (0,) * nd)


def _rms(x):
    return x * lax.rsqrt(jnp.mean(x * x, axis=-1, keepdims=True) + NORM_EPS)


def _dot_ones(z, ones):
    hi = z.astype(BF16)
    lo = (z - hi.astype(F32)).astype(BF16)
    both = jnp.dot(jnp.concatenate([hi, lo], axis=0), ones, preferred_element_type=F32)
    return both[:z.shape[0]] + both[z.shape[0]:]


def _mod_kernel(c_ref, w_ref, b_ref, o_ref):
    c = c_ref[...]
    s = c * jax.nn.sigmoid(c)
    o_ref[...] = jnp.dot(s.astype(BF16), w_ref[...].astype(BF16), preferred_element_type=F32) + b_ref[...]


def _mod_call(cin, w, b):
    rows, d = cin.shape
    n = w.shape[1]
    tn = 1024
    return pl.pallas_call(
        _mod_kernel,
        grid=(n // tn,),
        in_specs=[_const_spec((rows, d)), pl.BlockSpec((d, tn), lambda j: (0, j)), pl.BlockSpec((1, tn), lambda j: (0, j))],
        out_specs=pl.BlockSpec((rows, tn), lambda j: (0, j)),
        out_shape=jax.ShapeDtypeStruct((rows, n), F32),
        compiler_params=_cparams(("arbitrary",)),
        name="mod",
    )(cin, w, b)


def _prep_kernel(nt, ctx_ref, xm_ref, xp_ref, xn_ref, sh_ref, sc_ref, g0_ref, win_ref, mu_ref, kk_ref, ka_ref, rk_ref,
                 w0_ref, w2_ref, a0_ref, a2_ref, gw2_ref, cw_ref, gsum_ref,
                 r_o, kap_o, v_o, kdf_o, kdb_o, akf_o, akb_o, lwf_o, lwb_o, yc_o, gate_o, bon_o):
    i = pl.program_id(1)
    g0 = g0_ref[...]
    sh = sh_ref[0, 0]
    sc = sc_ref[0, 0]

    def norm_mod(x):
        return (_rms(x) * g0 * (1.0 + sc) + sh).astype(BF16)

    hb = norm_mod(jnp.where(i == 0, ctx_ref[0], xm_ref[0]))
    halo = norm_mod(jnp.concatenate([xp_ref[0], xn_ref[0]], axis=0))

    p_conv = jnp.dot(hb, win_ref[:, 0:3 * DR], preferred_element_type=F32)
    p_rkv = jnp.dot(hb, win_ref[:, 3 * DR:6 * DR], preferred_element_type=F32)
    p_lo = jnp.dot(hb, win_ref[:, 6 * DR:], preferred_element_type=F32)
    p_halo = jnp.dot(halo, win_ref[:, 3 * DR:6 * DR], preferred_element_type=F32)

    has_prev = (i >= 2).astype(F32)
    has_next = jnp.logical_and(i >= 1, i <= nt - 2).astype(F32)
    first = p_halo[7:8] * has_prev
    last = p_halo[8:9] * has_next

    rows = lax.broadcasted_iota(jnp.int32, (TM, 1), 0)

    def prev_of(z, row0):
        return jnp.where(rows == 0, row0, pltpu.roll(z, 1, axis=0))

    def next_of(z, rowl):
        return jnp.where(rows == TM - 1, rowl, pltpu.roll(z, TM - 1, axis=0))

    mu = mu_ref[...]

    def tshift(j):
        z = p_rkv[:, j * DR:(j + 1) * DR]
        zp = prev_of(z, first[:, j * DR:(j + 1) * DR])
        zn = next_of(z, last[:, j * DR:(j + 1) * DR])
        return z + mu[j:j + 1] * (zp - z) + mu[3 + j:4 + j] * (zn - z)

    r = tshift(0)
    k = tshift(1)
    v = tshift(2)

    gsum = gsum_ref[...]

    kk = k * kk_ref[...]
    sums = _dot_ones(jnp.concatenate([kk * kk, r * k * rk_ref[...]], axis=0), gsum)
    kap = kk * lax.rsqrt(sums[:TM] + 1e-12)
    bonus = sums[TM:] * v

    lo_w = jnp.tanh(p_lo[:, 128:256])
    zw = jnp.dot(lo_w, w2_ref[...], precision=HI, preferred_element_type=F32) + w0_ref[...]
    lw = -math.exp(-0.5) * jax.nn.sigmoid(zw)
    za = jnp.dot(p_lo[:, 256:384].astype(BF16), a2_ref[...], preferred_element_type=F32) + a0_ref[...]
    a = jax.nn.sigmoid(za)
    ka = ka_ref[...]

    r_o[0] = r
    kap_o[0] = kap
    v_o[0] = v
    for d, (kd_o, ak_o, lw_o) in enumerate(((kdf_o, akf_o, lwf_o), (kdb_o, akb_o, lwb_o))):
        a_d = a[:, d * DR:(d + 1) * DR]
        kd_o[0] = k * (1.0 + (a_d - 1.0) * ka)
        ak_o[0] = a_d * kap
        lw_o[0] = lw[:, d * DR:(d + 1) * DR]

    gate_o[0] = jnp.dot(jax.nn.sigmoid(p_lo[:, 0:128]).astype(BF16), gw2_ref[...], preferred_element_type=F32)
    bon_o[0] = bonus

    z = p_conv[:, DR:2 * DR] * p_conv[:, 2 * DR:3 * DR]
    col = rows % GRID_W
    zp = jnp.where(col == 0, 0.0, pltpu.roll(z, 1, axis=0))
    zn = jnp.where(col == GRID_W - 1, 0.0, pltpu.roll(z, TM - 1, axis=0))
    cw = cw_ref[...]
    yc_o[0] = p_conv[:, 0:DR] * (cw[0:1] * zp + cw[1:2] * z + cw[2:3] * zn)


def _prep_call(ctx, x, sh_all, sc_all, g0, win, mu6, k_k, k_a, r_k, w0, w2, a0, a2, gw2, cw, gsum):
    b, t, d = x.shape
    tt = t + ctx.shape[1]
    nt = tt // TM
    nb8 = t // 8
    per8 = TM // 8
    ctx_spec = pl.BlockSpec((1, TM, d), lambda bb, i: (bb, 0, 0))
    row_spec = pl.BlockSpec((1, TM, d), lambda bb, i: (bb, jnp.maximum(i - 1, 0), 0))
    prev_spec = pl.BlockSpec((1, 8, d), lambda bb, i: (bb, jnp.maximum((i - 1) * per8 - 1, 0), 0))
    next_spec = pl.BlockSpec((1, 8, d), lambda bb, i: (bb, jnp.clip(i * per8, 0, nb8 - 1), 0))
    mod_spec = pl.BlockSpec((1, 1, 1, d), lambda bb, i: (bb, jnp.minimum(i, 1), 0, 0))
    consts = (g0, win, mu6, k_k, k_a, r_k, w0, w2, a0, a2, gw2, cw, gsum)
    out_spec = pl.BlockSpec((1, TM, DR), lambda bb, i: (bb, i, 0))
    out_sds = jax.ShapeDtypeStruct((b, tt, DR), F32)
    return pl.pallas_call(
        functools.partial(_prep_kernel, nt),
        grid=(b, nt),
        in_specs=[ctx_spec, row_spec, prev_spec, next_spec, mod_spec, mod_spec] + [_const_spec(z.shape) for z in consts],
        out_specs=[out_spec] * 12,
        out_shape=[out_sds] * 12,
        compiler_params=_cparams(("arbitrary", "arbitrary")),
        name="prep",
    )(ctx, x, x, x, sh_all, sc_all, *consts)


def _scan_chunk(r, kap, v, kd, ak, lw, h, rev, bmask, bmask_f, tri):
    c = CHUNK
    t_idx = lax.broadcasted_iota(jnp.int32, (c, DR), 0)
    s_idx = lax.broadcasted_iota(jnp.int32, (c, DR), 1) % c
    if rev:
        strict = s_idx > t_idx
        incl = s_idx >= t_idx
    else:
        strict = s_idx < t_idx
        incl = s_idx <= t_idx
    eye = (s_idx == t_idx).astype(F32)

    cum = jnp.dot(tri, lw, precision=HI, preferred_element_type=F32)
    tot = cum[0:1] if rev else cum[c - 1:c]
    e_in = jnp.exp(cum)
    e_out = jnp.exp(-cum)
    e_rest = jnp.exp(tot - cum)
    kt = kap * jnp.exp(cum - lw)
    bh = ak * e_out
    kh = kd * e_out
    rt = r * e_in
    kbar = kd * e_rest
    bbar = ak * e_rest

    def bd(z):
        return jnp.concatenate([z.astype(BF16)] * NHEAD, axis=0) * bmask

    def unbd(full):
        return (full * bmask_f).reshape(NHEAD, HEAD, DR).sum(axis=0)

    def mm(lhs, rhs):
        return jnp.dot(lhs.astype(BF16), rhs, preferred_element_type=F32)

    amat = lax.dot_general(jnp.concatenate([kt, rt], axis=0).astype(BF16),
                           jnp.concatenate([bd(bh), bd(kh)], axis=0),
                           (((1,), (1,)), ((), ())), preferred_element_type=F32)
    n0 = jnp.where(strict, -amat[:c, :DR], 0.0)
    akk = jnp.where(strict, amat[:c, DR:], 0.0)
    arb = jnp.where(incl, amat[c:, :DR], 0.0)
    ark = jnp.where(incl, amat[c:, DR:], 0.0)

    p = n0
    tinv = eye + n0
    p = mm(p, bd(p))
    for _ in range(int(math.log2(c)) - 2):
        out = mm(jnp.concatenate([p, tinv], axis=0), bd(p))
        tinv = tinv + out[c:]
        p = out[:c]
    tinv = tinv + mm(tinv, bd(p))

    bdv = bd(v)
    av = mm(jnp.concatenate([akk, ark], axis=0), bdv)
    wu = mm(tinv, jnp.concatenate([bd(kt), bd(av[:c])], axis=1))
    w = wu[:, :DR]
    uloc = wu[:, DR:]

    lhs_t = jnp.concatenate([kbar, -bbar], axis=0).astype(BF16)
    rhs_t = jnp.concatenate([jnp.concatenate([v, uloc], axis=0),
                             jnp.concatenate([jnp.zeros_like(w), w], axis=0)], axis=1).astype(BF16)
    full = lax.dot_general(lhs_t, rhs_t, (((0,), (0,)), ((), ())), preferred_element_type=F32)
    gcat = unbd(full[:, :DR])
    corr = unbd(full[:, DR:])

    bwu = mm(arb, jnp.concatenate([bd(w), bd(uloc)], axis=1))
    rprime = rt - bwu[:, :DR]
    yloc = av[c:] - bwu[:, DR:]

    out = mm(jnp.concatenate([rprime, corr], axis=0), bd(h))
    pct = jnp.exp(jnp.broadcast_to(tot, (128, DR))).T
    low = lax.broadcasted_iota(jnp.int32, (HEAD, 128), 1) < HEAD
    pcb = jnp.concatenate([jnp.where(low, pct[2 * q * HEAD:(2 * q + 1) * HEAD], pct[(2 * q + 1) * HEAD:(2 * q + 2) * HEAD])
                           for q in range(NHEAD // 2)], axis=1)
    h_new = pcb * h + out[c:] + gcat
    return yloc + out[:c], h_new


def _scan_kernel(rf, kapf, vf, kdf, akf, lwf, rb, kapb, vb, kdb, akb, lwb, bm_ref, bmf_ref, trif_ref, trib_ref,
                 yf_o, yb_o, hf_s, hb_s):
    @pl.when(pl.program_id(1) == 0)
    def _():
        hf_s[...] = jnp.zeros_like(hf_s)
        hb_s[...] = jnp.zeros_like(hb_s)

    bm = bm_ref[...]
    bmf = bmf_ref[...]
    chains = []
    for i in range(rf.shape[0]):
        chains.append((yf_o, hf_s, i, (rf[i], kapf[i], vf[i], kdf[i], akf[i], lwf[i], hf_s[i], False, bm, bmf, trif_ref[...])))
        chains.append((yb_o, hb_s, i, (rb[i], kapb[i], vb[i], kdb[i], akb[i], lwb[i], hb_s[i], True, bm, bmf, trib_ref[...])))
    results = [_scan_chunk(*args) for (_, _, _, args) in chains]
    for (y_o, h_s, i, _), (y, h_new) in zip(chains, results):
        y_o[i] = y
        h_s[i] = h_new


def _scan_call(r, kap, v, kdf, kdb, akf, akb, lwf, lwb, n_ctx):
    b, tt, _ = r.shape
    nch = tt // CHUNK
    ncc = n_ctx // CHUNK
    hidx = jnp.arange(DR) // HEAD
    bmask_f = (hidx[:, None] == hidx[None, :]).astype(F32)
    ii = jnp.arange(CHUNK)
    tri_f = (ii[None, :] <= ii[:, None]).astype(F32)
    tri_b = (ii[None, :] >= ii[:, None]).astype(F32)

    def bwd_chunk(s):
        return jnp.where(s < ncc, ncc - 1 - s, nch - 1 - (s - ncc))

    nb = SCAN_NB if b % SCAN_NB == 0 else 1
    fspec = pl.BlockSpec((nb, CHUNK, DR), lambda bb, s: (bb, s, 0))
    bspec = pl.BlockSpec((nb, CHUNK, DR), lambda bb, s: (bb, bwd_chunk(s), 0))
    consts = (bmask_f.astype(BF16), bmask_f, tri_f, tri_b)
    sds = jax.ShapeDtypeStruct((b, tt, DR), F32)
    return pl.pallas_call(
        _scan_kernel,
        grid=(b // nb, nch),
        in_specs=[fspec] * 6 + [bspec] * 6 + [_const_spec(z.shape) for z in consts],
        out_specs=[fspec, bspec],
        out_shape=[sds, sds],
        scratch_shapes=[pltpu.VMEM((nb, HEAD, DR), F32), pltpu.VMEM((nb, HEAD, DR), F32)],
        compiler_params=_cparams(("arbitrary", "arbitrary")),
        name="scan",
    )(r, kap, v, kdf, akf, lwf, r, kap, v, kdb, akb, lwb, *consts)


def _post_kernel(yf_ref, yb_ref, bon_ref, gate_ref, yc_ref, x_ref, gt1_ref, sh2_ref, sc2_ref, g1_ref, g2_ref,
                 lg_ref, lb_ref, wout_ref, wq_ref, sk_ref, gmean_ref, x1_o, hx_o, st_o):
    gmean = gmean_ref[...]

    def group_mean(z):
        return _dot_ones(z, gmean) * (1.0 / HEAD)

    y = yf_ref[0] + yb_ref[0]
    dlt = y - group_mean(y)
    yn = dlt * lax.rsqrt(group_mean(dlt * dlt) + LNX_EPS) * lg_ref[...] + lb_ref[...]
    y_rwkv = (yn + bon_ref[0]) * gate_ref[0]
    cat = jnp.concatenate([yc_ref[0], y_rwkv], axis=-1).astype(BF16)
    o = jnp.dot(cat, wout_ref[...], preferred_element_type=F32)
    x1 = x_ref[0] + gt1_ref[0] * (_rms(o) * g1_ref[...])
    x1_o[0] = x1
    hx = _rms(x1) * g2_ref[...] * (1.0 + sc2_ref[0]) + sh2_ref[0]
    hx_o[0] = hx
    q = jnp.dot(hx.astype(BF16), wq_ref[...], preferred_element_type=F32).astype(BF16)
    for l in range(2 * PEER_HEADS):
        ql = q[:, l * 128:(l + 1) * 128]
        st_o[l] = lax.dot_general(sk_ref[l], ql, (((1,), (1,)), ((), ())), preferred_element_type=F32)


def _post_call(yf, yb, bonus, gate, yconv, x, gt1, sh2, sc2, g1, g2, lnx_g, lnx_b, wout, wq, sk, gmean, n_ctx):
    b, t, d = x.shape
    nt = t // TM
    off = n_ctx // TM
    mix_spec = pl.BlockSpec((1, TM, DR), lambda bb, i: (bb, i + off, 0))
    x_spec = pl.BlockSpec((1, TM, d), lambda bb, i: (bb, i, 0))
    mod_spec = pl.BlockSpec((1, 1, d), lambda bb, i: (bb, 0, 0))
    consts = (g1, g2, lnx_g, lnx_b, wout, wq, sk, gmean)
    nl = 2 * PEER_HEADS
    return pl.pallas_call(
        _post_kernel,
        grid=(b, nt),
        in_specs=[mix_spec] * 5 + [x_spec] + [mod_spec] * 3 + [_const_spec(z.shape) for z in consts],
        out_specs=[x_spec, x_spec, pl.BlockSpec((nl, NKEYS, TM), lambda bb, i: (0, 0, bb * nt + i))],
        out_shape=[jax.ShapeDtypeStruct((b, t, d), F32), jax.ShapeDtypeStruct((b, t, d), F32),
                   jax.ShapeDtypeStruct((nl, NKEYS, b * t), F32)],
        compiler_params=_cparams(("arbitrary", "arbitrary")),
        name="post",
    )(yf, yb, bonus, gate, yconv, x, gt1, sh2, sc2, *consts)


def _sel_step(xv, order, dep=0):
    big = jnp.int32(1 << 30)
    order = order + dep
    m = jnp.max(xv, axis=0, keepdims=True)
    am = jnp.min(jnp.where(xv == m, order, big), axis=0, keepdims=True)
    hit = order == am
    return m, am, hit, jnp.where(hit, -jnp.inf, xv)


def _top16(xv, order):
    vals, picks, hits = [], [], []
    for _ in range(TOPK):
        m, am, hit, xv = _sel_step(xv, order)
        vals.append(m)
        picks.append(am)
        hits.append(hit)
    return vals, picks, hits


def _candidates(vi, ii, vj, ij):
    tk = vi.shape[-1]
    crow = lax.broadcasted_iota(jnp.int32, (10 * 8, tk), 0)
    ca = jnp.where(crow < 16, 0, jnp.where(crow < 72, 1 + (crow - 16) // 8, crow - 64))
    cb = jnp.where(crow < 16, crow, jnp.where(crow < 72, (crow - 16) % 8, 0))
    stair = (ca + 1) * (cb + 1) <= TOPK
    order2 = ca * TOPK + cb
    cand = jnp.concatenate([vi[0:1] + vj] + [vi[a:a + 1] + vj[0:8] for a in range(1, 8)] + [vi[8:16] + vj[0:1]],
                           axis=0)
    eid = jnp.concatenate([ii[0:1] * NKEYS + ij] + [ii[a:a + 1] * NKEYS + ij[0:8] for a in range(1, 8)]
                          + [ii[8:16] * NKEYS + ij[0:1]], axis=0)
    return jnp.where(stair, cand, -jnp.inf), eid, order2


def _second_stage(vi, ii, vj, ij):
    cand, eid, order2 = _candidates(vi, ii, vj, ij)
    vals, _, hits = _top16(cand, order2)
    sc = jnp.concatenate(vals, axis=0)
    ex = jnp.concatenate([jnp.max(jnp.where(s, eid, -1), axis=0, keepdims=True) for s in hits], axis=0)
    e = jnp.exp(sc - sc[0:1])
    return ex, e / jnp.sum(e, axis=0, keepdims=True)


def _topk_head(s_i, s_j):
    rowi = lax.broadcasted_iota(jnp.int32, s_i.shape, 0)
    tops = []
    for s in (s_i, s_j):
        vals, picks, _ = _top16(s, rowi)
        tops.append((jnp.concatenate(vals, axis=0), jnp.concatenate(picks, axis=0)))
    (vi, ii), (vj, ij) = tops
    return _second_stage(vi, ii, vj, ij)


def _gelu(x):
    return 0.5 * x * (1.0 + lax.erf(x * (1.0 / math.sqrt(2.0))))


NSLOT = 4
TBLK = 2 * 8


def _peer_kernel(st0_ref, stn_ref, hx_ref, x1_ref, gt2_ref, g3_ref, uv_hbm, o_ref, *scratch):
    bufs = scratch[:NSLOT]
    sem, ids_sm, ids_v, g_s, sem_ids, idst, gt, vals_s, idx_s, acc = scratch[NSLOT:]
    tk, d = hx_ref.shape
    nc = d // 128
    step = pl.program_id(0)
    cur = step % 2
    nxt = 1 - cur

    def issue(half, t, slot):
        for p in range(NPAIR):
            e = ids_sm[half * tk + t, p]
            pltpu.make_async_copy(uv_hbm.at[e], bufs[slot].at[:, p, :], sem.at[slot]).start(priority=p % 2)

    def wait(slot):
        pltpu.make_async_copy(bufs[slot], bufs[slot], sem.at[slot]).wait()

    def publish(half):
        g_s[half] = gt[...].T
        ids_v[...] = idst[...].T
        cp = pltpu.make_async_copy(ids_v, ids_sm.at[pl.ds(half * tk, tk)], sem_ids)
        cp.start()
        cp.wait()

    def retrieve(st_ref, h):
        ex, g = _topk_head(st_ref[2 * h], st_ref[2 * h + 1])
        idst[pl.ds(h * TOPK, TOPK), :] = ex
        gt[pl.ds(h * TOPK, TOPK), :] = g

    @pl.when(step == 0)
    def _():
        for h in range(PEER_HEADS):
            retrieve(st0_ref, h)
        publish(0)
        vals_s[...] = jnp.zeros_like(vals_s)
        idx_s[...] = jnp.zeros_like(idx_s)
        for t in range(NSLOT):
            issue(0, t, t)

    rowi = lax.broadcasted_iota(jnp.int32, (8, NPAIR), 0)

    rowk = lax.broadcasted_iota(jnp.int32, (NKEYS, tk), 0)

    def block(m, last):
        s_i = stn_ref[2 * m]
        s_j = stn_ref[2 * m + 1]
        cand, eid, order2 = _candidates(vals_s[0], idx_s[0], vals_s[1], idx_s[1])
        out = {k: [] for k in ("vi", "ii", "vj", "ij", "sc", "ex")}
        for half8 in range(TBLK // 8):
            base = m * TBLK + half8 * 8
            if not last:
                base = pl.multiple_of(base, 8)
            x8 = hx_ref[pl.ds(base, 8), :].astype(BF16)
            g8 = g_s[cur, pl.ds(base, 8), :]
            o8 = jnp.zeros((8, d), F32)
            for j in range(8):
                slot = j % NSLOT
                wait(slot)
                u = jnp.concatenate([bufs[slot][s] for s in range(nc)], axis=1).astype(BF16)
                vv = jnp.concatenate([bufs[slot][nc + s] for s in range(nc)], axis=1).astype(BF16)
                act = lax.dot_general(x8, u, (((1,), (1,)), ((), ())), preferred_element_type=F32)
                w = jnp.where(rowi == j, g8 * _gelu(act), 0.0)
                o8 = o8 + jnp.dot(w.astype(BF16), vv, preferred_element_type=F32)
                if not last or half8 * 8 + j + NSLOT < TBLK:
                    issue(cur, base + j + NSLOT, slot)
                dep = ((lax.bitcast_convert_type(act[0:1], jnp.uint32) >> 16) >> 16).astype(jnp.int32)
                mi, ai, _, s_i = _sel_step(s_i, rowk, dep)
                mj, aj, _, s_j = _sel_step(s_j, rowk, dep)
                mc, _, hit, cand = _sel_step(cand, order2, dep)
                for k, val in (("vi", mi), ("ii", ai), ("vj", mj), ("ij", aj), ("sc", mc),
                               ("ex", jnp.max(jnp.where(hit, eid, -1), axis=0, keepdims=True))):
                    out[k].append(val)
            acc[pl.ds(base, 8), :] = o8
        prev = (m + PEER_HEADS - 1) % PEER_HEADS
        sc = jnp.concatenate(out["sc"], axis=0)
        e = jnp.exp(sc - sc[0:1])
        idst[pl.ds(prev * TOPK, TOPK), :] = jnp.concatenate(out["ex"], axis=0)
        gt[pl.ds(prev * TOPK, TOPK), :] = e / jnp.sum(e, axis=0, keepdims=True)
        vals_s[0] = jnp.concatenate(out["vi"], axis=0)
        idx_s[0] = jnp.concatenate(out["ii"], axis=0)
        vals_s[1] = jnp.concatenate(out["vj"], axis=0)
        idx_s[1] = jnp.concatenate(out["ij"], axis=0)

    def body(m, carry):
        block(m, False)
        return carry

    nblk = tk // TBLK
    lax.fori_loop(0, nblk - 1, body, 0)
    block(nblk - 1, True)
    ex, g = _second_stage(vals_s[0], idx_s[0], vals_s[1], idx_s[1])
    idst[pl.ds((PEER_HEADS - 1) * TOPK, TOPK), :] = ex
    gt[pl.ds((PEER_HEADS - 1) * TOPK, TOPK), :] = g
    publish(nxt)

    @pl.when(step + 1 < pl.num_programs(0))
    def _():
        for t in range(NSLOT):
            issue(nxt, t, t)

    o_ref[...] = x1_ref[...] + gt2_ref[0] * (_rms(acc[...]) * g3_ref[...])


def _peer_call(st, hx, x1, gt2, g3, uv, tokens_per_batch):
    n, d = hx.shape
    nl, nk, _ = st.shape
    assert nl == 2 * PEER_HEADS and nk == NKEYS and TK // TBLK == PEER_HEADS and NPAIR == 128
    nsteps = n // TK
    per_b = tokens_per_batch // TK
    tok_spec = pl.BlockSpec((TK, d), lambda i: (i, 0))
    return pl.pallas_call(
        _peer_kernel,
        grid=(nsteps,),
        in_specs=[pl.BlockSpec((nl, nk, TK), lambda i: (0, 0, 0)),
                  pl.BlockSpec((nl, nk, TK), lambda i: (0, 0, jnp.minimum(i + 1, nsteps - 1))),
                  tok_spec, tok_spec,
                  pl.BlockSpec((1, 1, d), lambda i: (i // per_b, 0, 0)),
                  _const_spec(g3.shape),
                  pl.BlockSpec(memory_space=pl.ANY)],
        out_specs=tok_spec,
        out_shape=jax.ShapeDtypeStruct((n, d), F32),
        scratch_shapes=[pltpu.VMEM((2 * d // 128, NPAIR, 128), F32) for _ in range(NSLOT)]
        + [pltpu.SemaphoreType.DMA((NSLOT,)),
           pltpu.SMEM((2 * TK, NPAIR), jnp.int32),
           pltpu.VMEM((TK, NPAIR), jnp.int32),
           pltpu.VMEM((2, TK, NPAIR), F32),
           pltpu.SemaphoreType.DMA(()),
           pltpu.VMEM((NPAIR, TK), jnp.int32),
           pltpu.VMEM((NPAIR, TK), F32),
           pltpu.VMEM((2, TOPK, TK), F32),
           pltpu.VMEM((2, TOPK, TK), jnp.int32),
           pltpu.VMEM((TK, d), F32)],
        compiler_params=_cparams(("arbitrary",)),
        name="peer",
    )(st, st, hx, x1, gt2, g3, uv)


def kernel(x, c, ctx, c_ctx, w_mod, b_mod, norm_gains, w_in, w_out, conv_w, tshift_mu, decay_w0, decay_w2, iclr_a0,
           iclr_a2, gate_w2, k_k, k_a, r_k, lnx_g, lnx_b, peer_wq, peer_subkeys, peer_u, peer_v):
    b, t, d = x.shape
    n_ctx = ctx.shape[1]
    assert w_mod.shape[0] == 1 and d == 2 * DR and t % TM == 0 and n_ctx == TM and b + 1 <= 8
    ng = norm_gains[0]

    cin = jnp.concatenate([c, c_ctx[None], jnp.zeros((8 - b - 1, d), F32)], axis=0)
    mod = _mod_call(cin, w_mod[0], b_mod[0][None])
    lat = mod[:b].reshape(b, 6, d)
    cm = mod[b].reshape(6, d)

    def both(j):
        return jnp.stack([jnp.broadcast_to(cm[j], (b, d)), lat[:, j]], axis=1)[:, :, None, :]

    zeros = jnp.zeros((HEAD, DR), F32)
    w2 = jnp.concatenate([jnp.concatenate([decay_w2[0, 0], zeros], axis=1),
                          jnp.concatenate([zeros, decay_w2[0, 1]], axis=1)], axis=0)
    a2 = jnp.concatenate([jnp.concatenate([iclr_a2[0, 0], zeros], axis=1),
                          jnp.concatenate([zeros, iclr_a2[0, 1]], axis=1)], axis=0)
    hidx = jnp.arange(DR) // HEAD
    gsum = (hidx[:, None] == hidx[None, :]).astype(BF16)

    feats = _prep_call(ctx, x, both(0), both(1), ng[0][None], w_in[0].astype(BF16), tshift_mu[0].reshape(6, DR),
                       k_k[0][None], k_a[0][None], r_k[0].reshape(1, DR), decay_w0[0].reshape(1, 2 * DR), w2,
                       iclr_a0[0].reshape(1, 2 * DR), a2.astype(BF16), gate_w2[0].astype(BF16), conv_w[0], gsum)
    r, kap, v, kdf, kdb, akf, akb, lwf, lwb, yconv, gate, bonus = feats

    yf, yb = _scan_call(r, kap, v, kdf, kdb, akf, akb, lwf, lwb, n_ctx)

    sk = peer_subkeys[0].reshape(2 * PEER_HEADS, NKEYS, -1).astype(BF16)
    x1, hx, st = _post_call(yf, yb, bonus, gate, yconv, x, lat[:, 2][:, None], lat[:, 3][:, None], lat[:, 4][:, None],
                            ng[1][None], ng[2][None], lnx_g[0][None], lnx_b[0][None], w_out[0].astype(BF16),
                            peer_wq[0].astype(BF16), sk, gsum, n_ctx)

    uv = jnp.concatenate([peer_u[0], peer_v[0]], axis=1).reshape(-1, 2 * d // 128, 128)
    out = _peer_call(st, hx.reshape(b * t, d), x1.reshape(b * t, d), lat[:, 5][:, None], ng[3][None], uv, t)
    return out.reshape(b, t, d)
```

```python
import functools
import math

import jax
import jax.numpy as jnp
from jax import lax
from jax.experimental import pallas as pl
from jax.experimental.pallas import tpu as pltpu

F32 = jnp.float32
BF16 = jnp.bfloat16
HI = lax.Precision.HIGHEST

NORM_EPS = 1e-6
LNX_EPS = 64e-5
HEAD = 64
NHEAD = 8
DR = HEAD * NHEAD
GRID_W = 64
CHUNK = 64
TOPK = 16
NKEYS = 128
PEER_HEADS = 8
NPAIR = PEER_HEADS * TOPK
TM = 256
TK = 128
SCAN_NB = 2
VMEM_LIMIT = 56 * 1024 * 1024


def _cparams(sem):
    return pltpu.CompilerParams(dimension_semantics=sem, vmem_limit_bytes=VMEM_LIMIT)


def _const_spec(shape):
    nd = len(shape)
    return pl.BlockSpec(shape, lambda *_: (0,) * nd)


def _rms(x):
    return x * lax.rsqrt(jnp.mean(x * x, axis=-1, keepdims=True) + NORM_EPS)


def _dot_ones(z, ones):
    hi = z.astype(BF16)
    lo = (z - hi.astype(F32)).astype(BF16)
    both = jnp.dot(jnp.concatenate([hi, lo], axis=0), ones, preferred_element_type=F32)
    return both[:z.shape[0]] + both[z.shape[0]:]


def _mod_kernel(c_ref, w_ref, b_ref, o_ref):
    c = c_ref[...]
    s = c * jax.nn.sigmoid(c)
    o_ref[...] = jnp.dot(s.astype(BF16), w_ref[...].astype(BF16), preferred_element_type=F32) + b_ref[...]


def _mod_call(cin, w, b):
    rows, d = cin.shape
    n = w.shape[1]
    tn = 1024
    return pl.pallas_call(
        _mod_kernel,
        grid=(n // tn,),
        in_specs=[_const_spec((rows, d)), pl.BlockSpec((d, tn), lambda j: (0, j)), pl.BlockSpec((1, tn), lambda j: (0, j))],
        out_specs=pl.BlockSpec((rows, tn), lambda j: (0, j)),
        out_shape=jax.ShapeDtypeStruct((rows, n), F32),
        compiler_params=_cparams(("arbitrary",)),
        name="mod",
    )(cin, w, b)


def _prep_kernel(nt, ctx_ref, xm_ref, xp_ref, xn_ref, sh_ref, sc_ref, g0_ref, win_ref, mu_ref, kk_ref, ka_ref, rk_ref,
                 w0_ref, w2_ref, a0_ref, a2_ref, gw2_ref, cw_ref, gsum_ref,
                 r_o, kap_o, v_o, kdf_o, kdb_o, akf_o, akb_o, lwf_o, lwb_o, yc_o, gate_o, bon_o):
    i = pl.program_id(1)
    g0 = g0_ref[...]
    sh = sh_ref[0, 0]
    sc = sc_ref[0, 0]

    def norm_mod(x):
        return (_rms(x) * g0 * (1.0 + sc) + sh).astype(BF16)

    hb = norm_mod(jnp.where(i == 0, ctx_ref[0], xm_ref[0]))
    halo = norm_mod(jnp.concatenate([xp_ref[0], xn_ref[0]], axis=0))

    p_conv = jnp.dot(hb, win_ref[:, 0:3 * DR], preferred_element_type=F32)
    p_rkv = jnp.dot(hb, win_ref[:, 3 * DR:6 * DR], preferred_element_type=F32)
    p_lo = jnp.dot(hb, win_ref[:, 6 * DR:], preferred_element_type=F32)
    p_halo = jnp.dot(halo, win_ref[:, 3 * DR:6 * DR], preferred_element_type=F32)

    has_prev = (i >= 2).astype(F32)
    has_next = jnp.logical_and(i >= 1, i <= nt - 2).astype(F32)
    first = p_halo[7:8] * has_prev
    last = p_halo[8:9] * has_next

    rows = lax.broadcasted_iota(jnp.int32, (TM, 1), 0)

    def prev_of(z, row0):
        return jnp.where(rows == 0, row0, pltpu.roll(z, 1, axis=0))

    def next_of(z, rowl):
        return jnp.where(rows == TM - 1, rowl, pltpu.roll(z, TM - 1, axis=0))

    mu = mu_ref[...]

    def tshift(j):
        z = p_rkv[:, j * DR:(j + 1) * DR]
        zp = prev_of(z, first[:, j * DR:(j + 1) * DR])
        zn = next_of(z, last[:, j * DR:(j + 1) * DR])
        return z + mu[j:j + 1] * (zp - z) + mu[3 + j:4 + j] * (zn - z)

    r = tshift(0)
    k = tshift(1)
    v = tshift(2)

    gsum = gsum_ref[...]

    kk = k * kk_ref[...]
    sums = _dot_ones(jnp.concatenate([kk * kk, r * k * rk_ref[...]], axis=0), gsum)
    kap = kk * lax.rsqrt(sums[:TM] + 1e-12)
    bonus = sums[TM:] * v

    lo_w = jnp.tanh(p_lo[:, 128:256])
    zw = jnp.dot(lo_w, w2_ref[...], precision=HI, preferred_element_type=F32) + w0_ref[...]
    lw = -math.exp(-0.5) * jax.nn.sigmoid(zw)
    za = jnp.dot(p_lo[:, 256:384].astype(BF16), a2_ref[...], preferred_element_type=F32) + a0_ref[...]
    a = jax.nn.sigmoid(za)
    ka = ka_ref[...]

    r_o[0] = r
    kap_o[0] = kap
    v_o[0] = v
    for d, (kd_o, ak_o, lw_o) in enumerate(((kdf_o, akf_o, lwf_o), (kdb_o, akb_o, lwb_o))):
        a_d = a[:, d * DR:(d + 1) * DR]
        kd_o[0] = k * (1.0 + (a_d - 1.0) * ka)
        ak_o[0] = a_d * kap
        lw_o[0] = lw[:, d * DR:(d + 1) * DR]

    gate_o[0] = jnp.dot(jax.nn.sigmoid(p_lo[:, 0:128]).astype(BF16), gw2_ref[...], preferred_element_type=F32)
    bon_o[0] = bonus

    z = p_conv[:, DR:2 * DR] * p_conv[:, 2 * DR:3 * DR]
    col = rows % GRID_W
    zp = jnp.where(col == 0, 0.0, pltpu.roll(z, 1, axis=0))
    zn = jnp.where(col == GRID_W - 1, 0.0, pltpu.roll(z, TM - 1, axis=0))
    cw = cw_ref[...]
    yc_o[0] = p_conv[:, 0:DR] * (cw[0:1] * zp + cw[1:2] * z + cw[2:3] * zn)


def _prep_call(ctx, x, sh_all, sc_all, g0, win, mu6, k_k, k_a, r_k, w0, w2, a0, a2, gw2, cw, gsum):
    b, t, d = x.shape
    tt = t + ctx.shape[1]
    nt = tt // TM
    nb8 = t // 8
    per8 = TM // 8
    ctx_spec = pl.BlockSpec((1, TM, d), lambda bb, i: (bb, 0, 0))
    row_spec = pl.BlockSpec((1, TM, d), lambda bb, i: (bb, jnp.maximum(i - 1, 0), 0))
    prev_spec = pl.BlockSpec((1, 8, d), lambda bb, i: (bb, jnp.maximum((i - 1) * per8 - 1, 0), 0))
    next_spec = pl.BlockSpec((1, 8, d), lambda bb, i: (bb, jnp.clip(i * per8, 0, nb8 - 1), 0))
    mod_spec = pl.BlockSpec((1, 1, 1, d), lambda bb, i: (bb, jnp.minimum(i, 1), 0, 0))
    consts = (g0, win, mu6, k_k, k_a, r_k, w0, w2, a0, a2, gw2, cw, gsum)
    out_spec = pl.BlockSpec((1, TM, DR), lambda bb, i: (bb, i, 0))
    out_sds = jax.ShapeDtypeStruct((b, tt, DR), F32)
    return pl.pallas_call(
        functools.partial(_prep_kernel, nt),
        grid=(b, nt),
        in_specs=[ctx_spec, row_spec, prev_spec, next_spec, mod_spec, mod_spec] + [_const_spec(z.shape) for z in consts],
        out_specs=[out_spec] * 12,
        out_shape=[out_sds] * 12,
        compiler_params=_cparams(("arbitrary", "arbitrary")),
        name="prep",
    )(ctx, x, x, x, sh_all, sc_all, *consts)


def _scan_chunk(r, kap, v, kd, ak, lw, h, rev, bmask, bmask_f, tri):
    c = CHUNK
    t_idx = lax.broadcasted_iota(jnp.int32, (c, DR), 0)
    s_idx = lax.broadcasted_iota(jnp.int32, (c, DR), 1) % c
    if rev:
        strict = s_idx > t_idx
        incl = s_idx >= t_idx
    else:
        strict = s_idx < t_idx
        incl = s_idx <= t_idx
    eye = (s_idx == t_idx).astype(F32)

    cum = jnp.dot(tri, lw, precision=HI, preferred_element_type=F32)
    tot = cum[0:1] if rev else cum[c - 1:c]
    e_in = jnp.exp(cum)
    e_out = jnp.exp(-cum)
    e_rest = jnp.exp(tot - cum)
    kt = kap * jnp.exp(cum - lw)
    bh = ak * e_out
    kh = kd * e_out
    rt = r * e_in
    kbar = kd * e_rest
    bbar = ak * e_rest

    def bd(z):
        return jnp.concatenate([z.astype(BF16)] * NHEAD, axis=0) * bmask

    def unbd(full):
        return (full * bmask_f).reshape(NHEAD, HEAD, DR).sum(axis=0)

    def mm(lhs, rhs):
        return jnp.dot(lhs.astype(BF16), rhs, preferred_element_type=F32)

    amat = lax.dot_general(jnp.concatenate([kt, rt], axis=0).astype(BF16),
                           jnp.concatenate([bd(bh), bd(kh)], axis=0),
                           (((1,), (1,)), ((), ())), preferred_element_type=F32)
    n0 = jnp.where(strict, -amat[:c, :DR], 0.0)
    akk = jnp.where(strict, amat[:c, DR:], 0.0)
    arb = jnp.where(incl, amat[c:, :DR], 0.0)
    ark = jnp.where(incl, amat[c:, DR:], 0.0)

    p = n0
    tinv = eye + n0
    p = mm(p, bd(p))
    for _ in range(int(math.log2(c)) - 2):
        out = mm(jnp.concatenate([p, tinv], axis=0), bd(p))
        tinv = tinv + out[c:]
        p = out[:c]
    tinv = tinv + mm(tinv, bd(p))

    bdv = bd(v)
    av = mm(jnp.concatenate([akk, ark], axis=0), bdv)
    wu = mm(tinv, jnp.concatenate([bd(kt), bd(av[:c])], axis=1))
    w = wu[:, :DR]
    uloc = wu[:, DR:]

    hw = mm(jnp.concatenate([rt, w], axis=0), bd(h))
    u = uloc + hw[c:]
    y = av[c:] + hw[:c] - mm(arb, bd(u))
    lhs_t = jnp.concatenate([kbar, -bbar], axis=0).astype(BF16)
    rhs_t = jnp.concatenate([v, u], axis=0).astype(BF16)
    full = lax.dot_general(lhs_t, rhs_t, (((0,), (0,)), ((), ())), preferred_element_type=F32)
    pct = jnp.exp(jnp.broadcast_to(tot, (128, DR))).T
    low = lax.broadcasted_iota(jnp.int32, (HEAD, 128), 1) < HEAD
    pcb = jnp.concatenate([jnp.where(low, pct[2 * q * HEAD:(2 * q + 1) * HEAD], pct[(2 * q + 1) * HEAD:(2 * q + 2) * HEAD])
                           for q in range(NHEAD // 2)], axis=1)
    h_new = pcb * h + unbd(full)
    return y, h_new


def _scan_kernel(rf, kapf, vf, kdf, akf, lwf, rb, kapb, vb, kdb, akb, lwb, bm_ref, bmf_ref, trif_ref, trib_ref,
                 yf_o, yb_o, hf_s, hb_s):
    @pl.when(pl.program_id(1) == 0)
    def _():
        hf_s[...] = jnp.zeros_like(hf_s)
        hb_s[...] = jnp.zeros_like(hb_s)

    bm = bm_ref[...]
    bmf = bmf_ref[...]
    chains = []
    for i in range(rf.shape[0]):
        chains.append((yf_o, hf_s, i, (rf[i], kapf[i], vf[i], kdf[i], akf[i], lwf[i], hf_s[i], False, bm, bmf, trif_ref[...])))
        chains.append((yb_o, hb_s, i, (rb[i], kapb[i], vb[i], kdb[i], akb[i], lwb[i], hb_s[i], True, bm, bmf, trib_ref[...])))
    results = [_scan_chunk(*args) for (_, _, _, args) in chains]
    for (y_o, h_s, i, _), (y, h_new) in zip(chains, results):
        y_o[i] = y
        h_s[i] = h_new


def _scan_call(r, kap, v, kdf, kdb, akf, akb, lwf, lwb, n_ctx):
    b, tt, _ = r.shape
    nch = tt // CHUNK
    ncc = n_ctx // CHUNK
    hidx = jnp.arange(DR) // HEAD
    bmask_f = (hidx[:, None] == hidx[None, :]).astype(F32)
    ii = jnp.arange(CHUNK)
    tri_f = (ii[None, :] <= ii[:, None]).astype(F32)
    tri_b = (ii[None, :] >= ii[:, None]).astype(F32)

    def bwd_chunk(s):
        return jnp.where(s < ncc, ncc - 1 - s, nch - 1 - (s - ncc))

    nb = SCAN_NB if b % SCAN_NB == 0 else 1
    fspec = pl.BlockSpec((nb, CHUNK, DR), lambda bb, s: (bb, s, 0))
    bspec = pl.BlockSpec((nb, CHUNK, DR), lambda bb, s: (bb, bwd_chunk(s), 0))
    consts = (bmask_f.astype(BF16), bmask_f, tri_f, tri_b)
    sds = jax.ShapeDtypeStruct((b, tt, DR), F32)
    return pl.pallas_call(
        _scan_kernel,
        grid=(b // nb, nch),
        in_specs=[fspec] * 6 + [bspec] * 6 + [_const_spec(z.shape) for z in consts],
        out_specs=[fspec, bspec],
        out_shape=[sds, sds],
        scratch_shapes=[pltpu.VMEM((nb, HEAD, DR), F32), pltpu.VMEM((nb, HEAD, DR), F32)],
        compiler_params=_cparams(("arbitrary", "arbitrary")),
        name="scan",
    )(r, kap, v, kdf, akf, lwf, r, kap, v, kdb, akb, lwb, *consts)


def _post_kernel(yf_ref, yb_ref, bon_ref, gate_ref, yc_ref, x_ref, gt1_ref, sh2_ref, sc2_ref, g1_ref, g2_ref,
                 lg_ref, lb_ref, wout_ref, wq_ref, sk_ref, gmean_ref, x1_o, hx_o, st_o):
    gmean = gmean_ref[...]

    def group_mean(z):
        return _dot_ones(z, gmean) * (1.0 / HEAD)

    y = yf_ref[0] + yb_ref[0]
    dlt = y - group_mean(y)
    yn = dlt * lax.rsqrt(group_mean(dlt * dlt) + LNX_EPS) * lg_ref[...] + lb_ref[...]
    y_rwkv = (yn + bon_ref[0]) * gate_ref[0]
    cat = jnp.concatenate([yc_ref[0], y_rwkv], axis=-1).astype(BF16)
    o = jnp.dot(cat, wout_ref[...], preferred_element_type=F32)
    x1 = x_ref[0] + gt1_ref[0] * (_rms(o) * g1_ref[...])
    x1_o[0] = x1
    hx = _rms(x1) * g2_ref[...] * (1.0 + sc2_ref[0]) + sh2_ref[0]
    hx_o[0] = hx
    q = jnp.dot(hx.astype(BF16), wq_ref[...], preferred_element_type=F32).astype(BF16)
    for l in range(2 * PEER_HEADS):
        ql = q[:, l * 128:(l + 1) * 128]
        st_o[l] = lax.dot_general(sk_ref[l], ql, (((1,), (1,)), ((), ())), preferred_element_type=F32)


def _post_call(yf, yb, bonus, gate, yconv, x, gt1, sh2, sc2, g1, g2, lnx_g, lnx_b, wout, wq, sk, gmean, n_ctx):
    b, t, d = x.shape
    nt = t // TM
    off = n_ctx // TM
    mix_spec = pl.BlockSpec((1, TM, DR), lambda bb, i: (bb, i + off, 0))
    x_spec = pl.BlockSpec((1, TM, d), lambda bb, i: (bb, i, 0))
    mod_spec = pl.BlockSpec((1, 1, d), lambda bb, i: (bb, 0, 0))
    consts = (g1, g2, lnx_g, lnx_b, wout, wq, sk, gmean)
    nl = 2 * PEER_HEADS
    return pl.pallas_call(
        _post_kernel,
        grid=(b, nt),
        in_specs=[mix_spec] * 5 + [x_spec] + [mod_spec] * 3 + [_const_spec(z.shape) for z in consts],
        out_specs=[x_spec, x_spec, pl.BlockSpec((nl, NKEYS, TM), lambda bb, i: (0, 0, bb * nt + i))],
        out_shape=[jax.ShapeDtypeStruct((b, t, d), F32), jax.ShapeDtypeStruct((b, t, d), F32),
                   jax.ShapeDtypeStruct((nl, NKEYS, b * t), F32)],
        compiler_params=_cparams(("arbitrary", "arbitrary")),
        name="post",
    )(yf, yb, bonus, gate, yconv, x, gt1, sh2, sc2, *consts)


def _sel_step(xv, order, dep=0):
    big = jnp.int32(1 << 30)
    order = order + dep
    m = jnp.max(xv, axis=0, keepdims=True)
    am = jnp.min(jnp.where(xv == m, order, big), axis=0, keepdims=True)
    hit = order == am
    return m, am, hit, jnp.where(hit, -jnp.inf, xv)


def _top16(xv, order):
    vals, picks, hits = [], [], []
    for _ in range(TOPK):
        m, am, hit, xv = _sel_step(xv, order)
        vals.append(m)
        picks.append(am)
        hits.append(hit)
    return vals, picks, hits


def _candidates(vi, ii, vj, ij):
    tk = vi.shape[-1]
    crow = lax.broadcasted_iota(jnp.int32, (10 * 8, tk), 0)
    ca = jnp.where(crow < 16, 0, jnp.where(crow < 72, 1 + (crow - 16) // 8, crow - 64))
    cb = jnp.where(crow < 16, crow, jnp.where(crow < 72, (crow - 16) % 8, 0))
    stair = (ca + 1) * (cb + 1) <= TOPK
    order2 = ca * TOPK + cb
    cand = jnp.concatenate([vi[0:1] + vj] + [vi[a:a + 1] + vj[0:8] for a in range(1, 8)] + [vi[8:16] + vj[0:1]],
                           axis=0)
    eid = jnp.concatenate([ii[0:1] * NKEYS + ij] + [ii[a:a + 1] * NKEYS + ij[0:8] for a in range(1, 8)]
                          + [ii[8:16] * NKEYS + ij[0:1]], axis=0)
    return jnp.where(stair, cand, -jnp.inf), eid, order2


def _second_stage(vi, ii, vj, ij):
    cand, eid, order2 = _candidates(vi, ii, vj, ij)
    vals, _, hits = _top16(cand, order2)
    sc = jnp.concatenate(vals, axis=0)
    ex = jnp.concatenate([jnp.max(jnp.where(s, eid, -1), axis=0, keepdims=True) for s in hits], axis=0)
    e = jnp.exp(sc - sc[0:1])
    return ex, e / jnp.sum(e, axis=0, keepdims=True)


def _topk_head(s_i, s_j):
    rowi = lax.broadcasted_iota(jnp.int32, s_i.shape, 0)
    tops = []
    for s in (s_i, s_j):
        vals, picks, _ = _top16(s, rowi)
        tops.append((jnp.concatenate(vals, axis=0), jnp.concatenate(picks, axis=0)))
    (vi, ii), (vj, ij) = tops
    return _second_stage(vi, ii, vj, ij)


def _gelu(x):
    return 0.5 * x * (1.0 + lax.erf(x * (1.0 / math.sqrt(2.0))))


NSLOT = 4
TBLK = 2 * 8


def _peer_kernel(st0_ref, stn_ref, hx_ref, x1_ref, gt2_ref, g3_ref, uv_hbm, o_ref, *scratch):
    bufs = scratch[:NSLOT]
    sem, ids_sm, ids_v, g_s, sem_ids, idst, gt, vals_s, idx_s, acc = scratch[NSLOT:]
    tk, d = hx_ref.shape
    nc = d // 128
    step = pl.program_id(0)
    cur = step % 2
    nxt = 1 - cur

    def issue(half, t, slot):
        for p in range(NPAIR):
            e = ids_sm[half * tk + t, p]
            pltpu.make_async_copy(uv_hbm.at[e], bufs[slot].at[:, p, :], sem.at[slot]).start(priority=p % 2)

    def wait(slot):
        pltpu.make_async_copy(bufs[slot], bufs[slot], sem.at[slot]).wait()

    def publish(half):
        g_s[half] = gt[...].T
        ids_v[...] = idst[...].T
        cp = pltpu.make_async_copy(ids_v, ids_sm.at[pl.ds(half * tk, tk)], sem_ids)
        cp.start()
        cp.wait()

    def retrieve(st_ref, h):
        ex, g = _topk_head(st_ref[2 * h], st_ref[2 * h + 1])
        idst[pl.ds(h * TOPK, TOPK), :] = ex
        gt[pl.ds(h * TOPK, TOPK), :] = g

    @pl.when(step == 0)
    def _():
        for h in range(PEER_HEADS):
            retrieve(st0_ref, h)
        publish(0)
        vals_s[...] = jnp.zeros_like(vals_s)
        idx_s[...] = jnp.zeros_like(idx_s)
        for t in range(NSLOT):
            issue(0, t, t)

    rowi = lax.broadcasted_iota(jnp.int32, (8, NPAIR), 0)

    rowk = lax.broadcasted_iota(jnp.int32, (NKEYS, tk), 0)

    def block(m, last):
        s_i = stn_ref[2 * m]
        s_j = stn_ref[2 * m + 1]
        cand, eid, order2 = _candidates(vals_s[0], idx_s[0], vals_s[1], idx_s[1])
        out = {k: [] for k in ("vi", "ii", "vj", "ij", "sc", "ex")}
        for half8 in range(TBLK // 8):
            base = m * TBLK + half8 * 8
            if not last:
                base = pl.multiple_of(base, 8)
            x8 = hx_ref[pl.ds(base, 8), :].astype(BF16)
            g8 = g_s[cur, pl.ds(base, 8), :]
            o8 = jnp.zeros((8, d), F32)
            for j in range(8):
                slot = j % NSLOT
                wait(slot)
                u = jnp.concatenate([bufs[slot][s] for s in range(nc)], axis=1).astype(BF16)
                vv = jnp.concatenate([bufs[slot][nc + s] for s in range(nc)], axis=1).astype(BF16)
                act = lax.dot_general(x8, u, (((1,), (1,)), ((), ())), preferred_element_type=F32)
                w = jnp.where(rowi == j, g8 * _gelu(act), 0.0)
                o8 = o8 + jnp.dot(w.astype(BF16), vv, preferred_element_type=F32)
                if not last or half8 * 8 + j + NSLOT < TBLK:
                    issue(cur, base + j + NSLOT, slot)
                dep = ((lax.bitcast_convert_type(act[0:1], jnp.uint32) >> 16) >> 16).astype(jnp.int32)
                mi, ai, _, s_i = _sel_step(s_i, rowk, dep)
                mj, aj, _, s_j = _sel_step(s_j, rowk, dep)
                mc, _, hit, cand = _sel_step(cand, order2, dep)
                for k, val in (("vi", mi), ("ii", ai), ("vj", mj), ("ij", aj), ("sc", mc),
                               ("ex", jnp.max(jnp.where(hit, eid, -1), axis=0, keepdims=True))):
                    out[k].append(val)
            acc[pl.ds(base, 8), :] = o8
        prev = (m + PEER_HEADS - 1) % PEER_HEADS
        sc = jnp.concatenate(out["sc"], axis=0)
        e = jnp.exp(sc - sc[0:1])
        idst[pl.ds(prev * TOPK, TOPK), :] = jnp.concatenate(out["ex"], axis=0)
        gt[pl.ds(prev * TOPK, TOPK), :] = e / jnp.sum(e, axis=0, keepdims=True)
        vals_s[0] = jnp.concatenate(out["vi"], axis=0)
        idx_s[0] = jnp.concatenate(out["ii"], axis=0)
        vals_s[1] = jnp.concatenate(out["vj"], axis=0)
        idx_s[1] = jnp.concatenate(out["ij"], axis=0)

    def body(m, carry):
        block(m, False)
        return carry

    nblk = tk // TBLK
    lax.fori_loop(0, nblk - 1, body, 0)
    block(nblk - 1, True)
    ex, g = _second_stage(vals_s[0], idx_s[0], vals_s[1], idx_s[1])
    idst[pl.ds((PEER_HEADS - 1) * TOPK, TOPK), :] = ex
    gt[pl.ds((PEER_HEADS - 1) * TOPK, TOPK), :] = g
    publish(nxt)

    @pl.when(step + 1 < pl.num_programs(0))
    def _():
        for t in range(NSLOT):
            issue(nxt, t, t)

    o_ref[...] = x1_ref[...] + gt2_ref[0] * (_rms(acc[...]) * g3_ref[...])


def _peer_call(st, hx, x1, gt2, g3, uv, tokens_per_batch):
    n, d = hx.shape
    nl, nk, _ = st.shape
    assert nl == 2 * PEER_HEADS and nk == NKEYS and TK // TBLK == PEER_HEADS and NPAIR == 128
    nsteps = n // TK
    per_b = tokens_per_batch // TK
    tok_spec = pl.BlockSpec((TK, d), lambda i: (i, 0))
    return pl.pallas_call(
        _peer_kernel,
        grid=(nsteps,),
        in_specs=[pl.BlockSpec((nl, nk, TK), lambda i: (0, 0, 0)),
                  pl.BlockSpec((nl, nk, TK), lambda i: (0, 0, jnp.minimum(i + 1, nsteps - 1))),
                  tok_spec, tok_spec,
                  pl.BlockSpec((1, 1, d), lambda i: (i // per_b, 0, 0)),
                  _const_spec(g3.shape),
                  pl.BlockSpec(memory_space=pl.ANY)],
        out_specs=tok_spec,
        out_shape=jax.ShapeDtypeStruct((n, d), F32),
        scratch_shapes=[pltpu.VMEM((2 * d // 128, NPAIR, 128), F32) for _ in range(NSLOT)]
        + [pltpu.SemaphoreType.DMA((NSLOT,)),
           pltpu.SMEM((2 * TK, NPAIR), jnp.int32),
           pltpu.VMEM((TK, NPAIR), jnp.int32),
           pltpu.VMEM((2, TK, NPAIR), F32),
           pltpu.SemaphoreType.DMA(()),
           pltpu.VMEM((NPAIR, TK), jnp.int32),
           pltpu.VMEM((NPAIR, TK), F32),
           pltpu.VMEM((2, TOPK, TK), F32),
           pltpu.VMEM((2, TOPK, TK), jnp.int32),
           pltpu.VMEM((TK, d), F32)],
        compiler_params=_cparams(("arbitrary",)),
        name="peer",
    )(st, st, hx, x1, gt2, g3, uv)


def kernel(x, c, ctx, c_ctx, w_mod, b_mod, norm_gains, w_in, w_out, conv_w, tshift_mu, decay_w0, decay_w2, iclr_a0,
           iclr_a2, gate_w2, k_k, k_a, r_k, lnx_g, lnx_b, peer_wq, peer_subkeys, peer_u, peer_v):
    b, t, d = x.shape
    n_ctx = ctx.shape[1]
    assert w_mod.shape[0] == 1 and d == 2 * DR and t % TM == 0 and n_ctx == TM and b + 1 <= 8
    ng = norm_gains[0]

    cin = jnp.concatenate([c, c_ctx[None], jnp.zeros((8 - b - 1, d), F32)], axis=0)
    mod = _mod_call(cin, w_mod[0], b_mod[0][None])
    lat = mod[:b].reshape(b, 6, d)
    cm = mod[b].reshape(6, d)

    def both(j):
        return jnp.stack([jnp.broadcast_to(cm[j], (b, d)), lat[:, j]], axis=1)[:, :, None, :]

    zeros = jnp.zeros((HEAD, DR), F32)
    w2 = jnp.concatenate([jnp.concatenate([decay_w2[0, 0], zeros], axis=1),
                          jnp.concatenate([zeros, decay_w2[0, 1]], axis=1)], axis=0)
    a2 = jnp.concatenate([jnp.concatenate([iclr_a2[0, 0], zeros], axis=1),
                          jnp.concatenate([zeros, iclr_a2[0, 1]], axis=1)], axis=0)
    hidx = jnp.arange(DR) // HEAD
    gsum = (hidx[:, None] == hidx[None, :]).astype(BF16)

    feats = _prep_call(ctx, x, both(0), both(1), ng[0][None], w_in[0].astype(BF16), tshift_mu[0].reshape(6, DR),
                       k_k[0][None], k_a[0][None], r_k[0].reshape(1, DR), decay_w0[0].reshape(1, 2 * DR), w2,
                       iclr_a0[0].reshape(1, 2 * DR), a2.astype(BF16), gate_w2[0].astype(BF16), conv_w[0], gsum)
    r, kap, v, kdf, kdb, akf, akb, lwf, lwb, yconv, gate, bonus = feats

    yf, yb = _scan_call(r, kap, v, kdf, kdb, akf, akb, lwf, lwb, n_ctx)

    sk = peer_subkeys[0].reshape(2 * PEER_HEADS, NKEYS, -1).astype(BF16)
    x1, hx, st = _post_call(yf, yb, bonus, gate, yconv, x, lat[:, 2][:, None], lat[:, 3][:, None], lat[:, 4][:, None],
                            ng[1][None], ng[2][None], lnx_g[0][None], lnx_b[0][None], w_out[0].astype(BF16),
                            peer_wq[0].astype(BF16), sk, gsum, n_ctx)

    uv = jnp.concatenate([peer_u[0], peer_v[0]], axis=1).reshape(-1, 2 * d // 128, 128)
    out = _peer_call(st, hx.reshape(b * t, d), x1.reshape(b * t, d), lat[:, 5][:, None], ng[3][None], uv, t)
    return out.reshape(b, t, d)
```

```python
import functools
import math

import jax
import jax.numpy as jnp
from jax import lax
from jax.experimental import pallas as pl
from jax.experimental.pallas import tpu as pltpu

F32 = jnp.float32
BF16 = jnp.bfloat16
HI = lax.Precision.HIGHEST

NORM_EPS = 1e-6
LNX_EPS = 64e-5
HEAD = 64
NHEAD = 8
DR = HEAD * NHEAD
GRID_W = 64
CHUNK = 64
TOPK = 16
NKEYS = 128
PEER_HEADS = 8
NPAIR = PEER_HEADS * TOPK
TM = 256
TK = 128
SCAN_NB = 4
VMEM_LIMIT = 56 * 1024 * 1024


def _cparams(sem):
    return pltpu.CompilerParams(dimension_semantics=sem, vmem_limit_bytes=VMEM_LIMIT)


def _const_spec(shape):
    nd = len(shape)
    return pl.BlockSpec(shape, lambda *_: (0,) * nd)


def _rms(x):
    return x * lax.rsqrt(jnp.mean(x * x, axis=-1, keepdims=True) + NORM_EPS)


def _dot_ones(z, ones):
    hi = z.astype(BF16)
    lo = (z - hi.astype(F32)).astype(BF16)
    both = jnp.dot(jnp.concatenate([hi, lo], axis=0), ones, preferred_element_type=F32)
    return both[:z.shape[0]] + both[z.shape[0]:]


def _mod_kernel(c_ref, w_ref, b_ref, o_ref):
    c = c_ref[...]
    s = c * jax.nn.sigmoid(c)
    o_ref[...] = jnp.dot(s.astype(BF16), w_ref[...].astype(BF16), preferred_element_type=F32) + b_ref[...]


def _mod_call(cin, w, b):
    rows, d = cin.shape
    n = w.shape[1]
    tn = 1024
    return pl.pallas_call(
        _mod_kernel,
        grid=(n // tn,),
        in_specs=[_const_spec((rows, d)), pl.BlockSpec((d, tn), lambda j: (0, j)), pl.BlockSpec((1, tn), lambda j: (0, j))],
        out_specs=pl.BlockSpec((rows, tn), lambda j: (0, j)),
        out_shape=jax.ShapeDtypeStruct((rows, n), F32),
        compiler_params=_cparams(("arbitrary",)),
        name="mod",
    )(cin, w, b)


def _prep_kernel(nt, ctx_ref, xm_ref, xp_ref, xn_ref, sh_ref, sc_ref, g0_ref, win_ref, mu_ref, kk_ref, ka_ref, rk_ref,
                 w0_ref, w2_ref, a0_ref, a2_ref, gw2_ref, cw_ref, gsum_ref,
                 r_o, kap_o, v_o, kdf_o, kdb_o, akf_o, akb_o, lwf_o, lwb_o, yc_o, gate_o, bon_o):
    i = pl.program_id(1)
    g0 = g0_ref[...]
    sh = sh_ref[0, 0]
    sc = sc_ref[0, 0]

    def norm_mod(x):
        return (_rms(x) * g0 * (1.0 + sc) + sh).astype(BF16)

    hb = norm_mod(jnp.where(i == 0, ctx_ref[0], xm_ref[0]))
    halo = norm_mod(jnp.concatenate([xp_ref[0], xn_ref[0]], axis=0))

    p_conv = jnp.dot(hb, win_ref[:, 0:3 * DR], preferred_element_type=F32)
    p_rkv = jnp.dot(hb, win_ref[:, 3 * DR:6 * DR], preferred_element_type=F32)
    p_lo = jnp.dot(hb, win_ref[:, 6 * DR:], preferred_element_type=F32)
    p_halo = jnp.dot(halo, win_ref[:, 3 * DR:6 * DR], preferred_element_type=F32)

    has_prev = (i >= 2).astype(F32)
    has_next = jnp.logical_and(i >= 1, i <= nt - 2).astype(F32)
    first = p_halo[7:8] * has_prev
    last = p_halo[8:9] * has_next

    rows = lax.broadcasted_iota(jnp.int32, (TM, 1), 0)

    def prev_of(z, row0):
        return jnp.where(rows == 0, row0, pltpu.roll(z, 1, axis=0))

    def next_of(z, rowl):
        return jnp.where(rows == TM - 1, rowl, pltpu.roll(z, TM - 1, axis=0))

    mu = mu_ref[...]

    def tshift(j):
        z = p_rkv[:, j * DR:(j + 1) * DR]
        zp = prev_of(z, first[:, j * DR:(j + 1) * DR])
        zn = next_of(z, last[:, j * DR:(j + 1) * DR])
        return z + mu[j:j + 1] * (zp - z) + mu[3 + j:4 + j] * (zn - z)

    r = tshift(0)
    k = tshift(1)
    v = tshift(2)

    gsum = gsum_ref[...]

    kk = k * kk_ref[...]
    sums = _dot_ones(jnp.concatenate([kk * kk, r * k * rk_ref[...]], axis=0), gsum)
    kap = kk * lax.rsqrt(sums[:TM] + 1e-12)
    bonus = sums[TM:] * v

    lo_w = jnp.tanh(p_lo[:, 128:256])
    zw = jnp.dot(lo_w, w2_ref[...], precision=HI, preferred_element_type=F32) + w0_ref[...]
    lw = -math.exp(-0.5) * jax.nn.sigmoid(zw)
    za = jnp.dot(p_lo[:, 256:384].astype(BF16), a2_ref[...], preferred_element_type=F32) + a0_ref[...]
    a = jax.nn.sigmoid(za)
    ka = ka_ref[...]

    r_o[0] = r
    kap_o[0] = kap
    v_o[0] = v
    for d, (kd_o, ak_o, lw_o) in enumerate(((kdf_o, akf_o, lwf_o), (kdb_o, akb_o, lwb_o))):
        a_d = a[:, d * DR:(d + 1) * DR]
        kd_o[0] = k * (1.0 + (a_d - 1.0) * ka)
        ak_o[0] = a_d * kap
        lw_o[0] = lw[:, d * DR:(d + 1) * DR]

    gate_o[0] = jnp.dot(jax.nn.sigmoid(p_lo[:, 0:128]).astype(BF16), gw2_ref[...], preferred_element_type=F32)
    bon_o[0] = bonus

    z = p_conv[:, DR:2 * DR] * p_conv[:, 2 * DR:3 * DR]
    col = rows % GRID_W
    zp = jnp.where(col == 0, 0.0, pltpu.roll(z, 1, axis=0))
    zn = jnp.where(col == GRID_W - 1, 0.0, pltpu.roll(z, TM - 1, axis=0))
    cw = cw_ref[...]
    yc_o[0] = p_conv[:, 0:DR] * (cw[0:1] * zp + cw[1:2] * z + cw[2:3] * zn)


def _prep_call(ctx, x, sh_all, sc_all, g0, win, mu6, k_k, k_a, r_k, w0, w2, a0, a2, gw2, cw, gsum):
    b, t, d = x.shape
    tt = t + ctx.shape[1]
    nt = tt // TM
    nb8 = t // 8
    per8 = TM // 8
    ctx_spec = pl.BlockSpec((1, TM, d), lambda bb, i: (bb, 0, 0))
    row_spec = pl.BlockSpec((1, TM, d), lambda bb, i: (bb, jnp.maximum(i - 1, 0), 0))
    prev_spec = pl.BlockSpec((1, 8, d), lambda bb, i: (bb, jnp.maximum((i - 1) * per8 - 1, 0), 0))
    next_spec = pl.BlockSpec((1, 8, d), lambda bb, i: (bb, jnp.clip(i * per8, 0, nb8 - 1), 0))
    mod_spec = pl.BlockSpec((1, 1, 1, d), lambda bb, i: (bb, jnp.minimum(i, 1), 0, 0))
    consts = (g0, win, mu6, k_k, k_a, r_k, w0, w2, a0, a2, gw2, cw, gsum)
    out_spec = pl.BlockSpec((1, TM, DR), lambda bb, i: (bb, i, 0))
    out_sds = jax.ShapeDtypeStruct((b, tt, DR), F32)
    return pl.pallas_call(
        functools.partial(_prep_kernel, nt),
        grid=(b, nt),
        in_specs=[ctx_spec, row_spec, prev_spec, next_spec, mod_spec, mod_spec] + [_const_spec(z.shape) for z in consts],
        out_specs=[out_spec] * 12,
        out_shape=[out_sds] * 12,
        compiler_params=_cparams(("arbitrary", "arbitrary")),
        name="prep",
    )(ctx, x, x, x, sh_all, sc_all, *consts)


def _scan_chunk(r, kap, v, kd, ak, lw, h, rev, bmask, bmask_f, tri):
    c = CHUNK
    t_idx = lax.broadcasted_iota(jnp.int32, (c, DR), 0)
    s_idx = lax.broadcasted_iota(jnp.int32, (c, DR), 1) % c
    if rev:
        strict = s_idx > t_idx
        incl = s_idx >= t_idx
    else:
        strict = s_idx < t_idx
        incl = s_idx <= t_idx
    eye = (s_idx == t_idx).astype(F32)

    cum = jnp.dot(tri, lw, precision=HI, preferred_element_type=F32)
    yield
    tot = cum[0:1] if rev else cum[c - 1:c]
    e_in = jnp.exp(cum)
    e_out = jnp.exp(-cum)
    e_rest = jnp.exp(tot - cum)
    kt = kap * jnp.exp(cum - lw)
    bh = ak * e_out
    kh = kd * e_out
    rt = r * e_in
    kbar = kd * e_rest
    bbar = ak * e_rest

    def bd(z):
        return jnp.concatenate([z.astype(BF16)] * NHEAD, axis=0) * bmask

    def unbd(full):
        return (full * bmask_f).reshape(NHEAD, HEAD, DR).sum(axis=0)

    def mm(lhs, rhs):
        return jnp.dot(lhs.astype(BF16), rhs, preferred_element_type=F32)

    amat = lax.dot_general(jnp.concatenate([kt, rt], axis=0).astype(BF16),
                           jnp.concatenate([bd(bh), bd(kh)], axis=0),
                           (((1,), (1,)), ((), ())), preferred_element_type=F32)
    yield
    n0 = jnp.where(strict, -amat[:c, :DR], 0.0)
    akk = jnp.where(strict, amat[:c, DR:], 0.0)
    arb = jnp.where(incl, amat[c:, :DR], 0.0)
    ark = jnp.where(incl, amat[c:, DR:], 0.0)

    p = n0
    tinv = eye + n0
    bdv = bd(v)
    av = mm(jnp.concatenate([akk, ark], axis=0), bdv)
    p = mm(p, bd(p))
    yield
    for _ in range(int(math.log2(c)) - 2):
        out = mm(jnp.concatenate([p, tinv], axis=0), bd(p))
        yield
        tinv = tinv + out[c:]
        p = out[:c]
    tinv = tinv + mm(tinv, bd(p))
    yield
    wu = mm(tinv, jnp.concatenate([bd(kt), bd(av[:c])], axis=1))
    yield
    w = wu[:, :DR]
    uloc = wu[:, DR:]

    hw = mm(jnp.concatenate([rt, w], axis=0), bd(h))
    yield
    u = uloc + hw[c:]
    y = av[c:] + hw[:c] - mm(arb, bd(u))
    lhs_t = jnp.concatenate([kbar, -bbar], axis=0).astype(BF16)
    rhs_t = jnp.concatenate([v, u], axis=0).astype(BF16)
    full = lax.dot_general(lhs_t, rhs_t, (((0,), (0,)), ((), ())), preferred_element_type=F32)
    yield
    pct = jnp.exp(jnp.broadcast_to(tot, (128, DR))).T
    low = lax.broadcasted_iota(jnp.int32, (HEAD, 128), 1) < HEAD
    pcb = jnp.concatenate([jnp.where(low, pct[2 * q * HEAD:(2 * q + 1) * HEAD], pct[(2 * q + 1) * HEAD:(2 * q + 2) * HEAD])
                           for q in range(NHEAD // 2)], axis=1)
    h_new = pcb * h + unbd(full)
    return y, h_new


def _interleave(gens):
    results = [None] * len(gens)
    live = list(enumerate(gens))
    while live:
        still = []
        for i, g in live:
            try:
                next(g)
                still.append((i, g))
            except StopIteration as stop:
                results[i] = stop.value
        live = still
    return results


def _scan_kernel(rf, kapf, vf, kdf, akf, lwf, rb, kapb, vb, kdb, akb, lwb, bm_ref, bmf_ref, trif_ref, trib_ref,
                 yf_o, yb_o, hf_s, hb_s):
    @pl.when(pl.program_id(1) == 0)
    def _():
        hf_s[...] = jnp.zeros_like(hf_s)
        hb_s[...] = jnp.zeros_like(hb_s)

    bm = bm_ref[...]
    bmf = bmf_ref[...]
    chains = []
    for i in range(rf.shape[0]):
        chains.append((yf_o, hf_s, i, (rf[i], kapf[i], vf[i], kdf[i], akf[i], lwf[i], hf_s[i], False, bm, bmf, trif_ref[...])))
        chains.append((yb_o, hb_s, i, (rb[i], kapb[i], vb[i], kdb[i], akb[i], lwb[i], hb_s[i], True, bm, bmf, trib_ref[...])))
    results = _interleave([_scan_chunk(*args) for (_, _, _, args) in chains])
    for (y_o, h_s, i, _), (y, h_new) in zip(chains, results):
        y_o[i] = y
        h_s[i] = h_new


def _scan_call(r, kap, v, kdf, kdb, akf, akb, lwf, lwb, n_ctx):
    b, tt, _ = r.shape
    nch = tt // CHUNK
    ncc = n_ctx // CHUNK
    hidx = jnp.arange(DR) // HEAD
    bmask_f = (hidx[:, None] == hidx[None, :]).astype(F32)
    ii = jnp.arange(CHUNK)
    tri_f = (ii[None, :] <= ii[:, None]).astype(F32)
    tri_b = (ii[None, :] >= ii[:, None]).astype(F32)

    def bwd_chunk(s):
        return jnp.where(s < ncc, ncc - 1 - s, nch - 1 - (s - ncc))

    nb = SCAN_NB if b % SCAN_NB == 0 else 1
    fspec = pl.BlockSpec((nb, CHUNK, DR), lambda bb, s: (bb, s, 0))
    bspec = pl.BlockSpec((nb, CHUNK, DR), lambda bb, s: (bb, bwd_chunk(s), 0))
    consts = (bmask_f.astype(BF16), bmask_f, tri_f, tri_b)
    sds = jax.ShapeDtypeStruct((b, tt, DR), F32)
    return pl.pallas_call(
        _scan_kernel,
        grid=(b // nb, nch),
        in_specs=[fspec] * 6 + [bspec] * 6 + [_const_spec(z.shape) for z in consts],
        out_specs=[fspec, bspec],
        out_shape=[sds, sds],
        scratch_shapes=[pltpu.VMEM((nb, HEAD, DR), F32), pltpu.VMEM((nb, HEAD, DR), F32)],
        compiler_params=_cparams(("arbitrary", "arbitrary")),
        name="scan",
    )(r, kap, v, kdf, akf, lwf, r, kap, v, kdb, akb, lwb, *consts)


def _post_kernel(yf_ref, yb_ref, bon_ref, gate_ref, yc_ref, x_ref, gt1_ref, sh2_ref, sc2_ref, g1_ref, g2_ref,
                 lg_ref, lb_ref, wout_ref, wq_ref, sk_ref, gmean_ref, x1_o, hx_o, st_o):
    gmean = gmean_ref[...]

    def group_mean(z):
        return _dot_ones(z, gmean) * (1.0 / HEAD)

    y = yf_ref[0] + yb_ref[0]
    dlt = y - group_mean(y)
    yn = dlt * lax.rsqrt(group_mean(dlt * dlt) + LNX_EPS) * lg_ref[...] + lb_ref[...]
    y_rwkv = (yn + bon_ref[0]) * gate_ref[0]
    cat = jnp.concatenate([yc_ref[0], y_rwkv], axis=-1).astype(BF16)
    o = jnp.dot(cat, wout_ref[...], preferred_element_type=F32)
    x1 = x_ref[0] + gt1_ref[0] * (_rms(o) * g1_ref[...])
    x1_o[0] = x1
    hx = _rms(x1) * g2_ref[...] * (1.0 + sc2_ref[0]) + sh2_ref[0]
    hx_o[0] = hx
    q = jnp.dot(hx.astype(BF16), wq_ref[...], preferred_element_type=F32).astype(BF16)
    for l in range(2 * PEER_HEADS):
        ql = q[:, l * 128:(l + 1) * 128]
        st_o[l] = lax.dot_general(sk_ref[l], ql, (((1,), (1,)), ((), ())), preferred_element_type=F32)


def _post_call(yf, yb, bonus, gate, yconv, x, gt1, sh2, sc2, g1, g2, lnx_g, lnx_b, wout, wq, sk, gmean, n_ctx):
    b, t, d = x.shape
    nt = t // TM
    off = n_ctx // TM
    mix_spec = pl.BlockSpec((1, TM, DR), lambda bb, i: (bb, i + off, 0))
    x_spec = pl.BlockSpec((1, TM, d), lambda bb, i: (bb, i, 0))
    mod_spec = pl.BlockSpec((1, 1, d), lambda bb, i: (bb, 0, 0))
    consts = (g1, g2, lnx_g, lnx_b, wout, wq, sk, gmean)
    nl = 2 * PEER_HEADS
    return pl.pallas_call(
        _post_kernel,
        grid=(b, nt),
        in_specs=[mix_spec] * 5 + [x_spec] + [mod_spec] * 3 + [_const_spec(z.shape) for z in consts],
        out_specs=[x_spec, x_spec, pl.BlockSpec((nl, NKEYS, TM), lambda bb, i: (0, 0, bb * nt + i))],
        out_shape=[jax.ShapeDtypeStruct((b, t, d), F32), jax.ShapeDtypeStruct((b, t, d), F32),
                   jax.ShapeDtypeStruct((nl, NKEYS, b * t), F32)],
        compiler_params=_cparams(("arbitrary", "arbitrary")),
        name="post",
    )(yf, yb, bonus, gate, yconv, x, gt1, sh2, sc2, *consts)


def _sel_step(xv, order, dep=0):
    big = jnp.int32(1 << 30)
    order = order + dep
    m = jnp.max(xv, axis=0, keepdims=True)
    am = jnp.min(jnp.where(xv == m, order, big), axis=0, keepdims=True)
    hit = order == am
    return m, am, hit, jnp.where(hit, -jnp.inf, xv)


def _top16(xv, order):
    vals, picks, hits = [], [], []
    for _ in range(TOPK):
        m, am, hit, xv = _sel_step(xv, order)
        vals.append(m)
        picks.append(am)
        hits.append(hit)
    return vals, picks, hits


def _candidates(vi, ii, vj, ij):
    tk = vi.shape[-1]
    crow = lax.broadcasted_iota(jnp.int32, (10 * 8, tk), 0)
    ca = jnp.where(crow < 16, 0, jnp.where(crow < 72, 1 + (crow - 16) // 8, crow - 64))
    cb = jnp.where(crow < 16, crow, jnp.where(crow < 72, (crow - 16) % 8, 0))
    stair = (ca + 1) * (cb + 1) <= TOPK
    order2 = ca * TOPK + cb
    cand = jnp.concatenate([vi[0:1] + vj] + [vi[a:a + 1] + vj[0:8] for a in range(1, 8)] + [vi[8:16] + vj[0:1]],
                           axis=0)
    eid = jnp.concatenate([ii[0:1] * NKEYS + ij] + [ii[a:a + 1] * NKEYS + ij[0:8] for a in range(1, 8)]
                          + [ii[8:16] * NKEYS + ij[0:1]], axis=0)
    return jnp.where(stair, cand, -jnp.inf), eid, order2


def _second_stage(vi, ii, vj, ij):
    cand, eid, order2 = _candidates(vi, ii, vj, ij)
    vals, _, hits = _top16(cand, order2)
    sc = jnp.concatenate(vals, axis=0)
    ex = jnp.concatenate([jnp.max(jnp.where(s, eid, -1), axis=0, keepdims=True) for s in hits], axis=0)
    e = jnp.exp(sc - sc[0:1])
    return ex, e / jnp.sum(e, axis=0, keepdims=True)


def _topk_head(s_i, s_j):
    rowi = lax.broadcasted_iota(jnp.int32, s_i.shape, 0)
    tops = []
    for s in (s_i, s_j):
        vals, picks, _ = _top16(s, rowi)
        tops.append((jnp.concatenate(vals, axis=0), jnp.concatenate(picks, axis=0)))
    (vi, ii), (vj, ij) = tops
    return _second_stage(vi, ii, vj, ij)


def _gelu(x):
    return 0.5 * x * (1.0 + lax.erf(x * (1.0 / math.sqrt(2.0))))


NSLOT = 4
TBLK = 2 * 8


def _peer_kernel(st0_ref, stn_ref, hx_ref, x1_ref, gt2_ref, g3_ref, uv_hbm, o_ref, *scratch):
    bufs = scratch[:NSLOT]
    sem, ids_sm, ids_v, g_s, sem_ids, idst, gt, vals_s, idx_s, acc = scratch[NSLOT:]
    tk, d = hx_ref.shape
    nc = d // 128
    step = pl.program_id(0)
    cur = step % 2
    nxt = 1 - cur

    def issue(half, t, slot):
        for p in range(NPAIR):
            e = ids_sm[half * tk + t, p]
            pltpu.make_async_copy(uv_hbm.at[e], bufs[slot].at[:, p, :], sem.at[slot]).start(priority=p % 2)

    def wait(slot):
        pltpu.make_async_copy(bufs[slot], bufs[slot], sem.at[slot]).wait()

    def publish(half):
        g_s[half] = gt[...].T
        ids_v[...] = idst[...].T
        cp = pltpu.make_async_copy(ids_v, ids_sm.at[pl.ds(half * tk, tk)], sem_ids)
        cp.start()
        cp.wait()

    def retrieve(st_ref, h):
        ex, g = _topk_head(st_ref[2 * h], st_ref[2 * h + 1])
        idst[pl.ds(h * TOPK, TOPK), :] = ex
        gt[pl.ds(h * TOPK, TOPK), :] = g

    @pl.when(step == 0)
    def _():
        for h in range(PEER_HEADS):
            retrieve(st0_ref, h)
        publish(0)
        vals_s[...] = jnp.zeros_like(vals_s)
        idx_s[...] = jnp.zeros_like(idx_s)
        for t in range(NSLOT):
            issue(0, t, t)

    rowi = lax.broadcasted_iota(jnp.int32, (8, NPAIR), 0)

    rowk = lax.broadcasted_iota(jnp.int32, (NKEYS, tk), 0)

    def block(m, last):
        s_i = stn_ref[2 * m]
        s_j = stn_ref[2 * m + 1]
        cand, eid, order2 = _candidates(vals_s[0], idx_s[0], vals_s[1], idx_s[1])
        out = {k: [] for k in ("vi", "ii", "vj", "ij", "sc", "ex")}
        for half8 in range(TBLK // 8):
            base = m * TBLK + half8 * 8
            if not last:
                base = pl.multiple_of(base, 8)
            x8 = hx_ref[pl.ds(base, 8), :].astype(BF16)
            g8 = g_s[cur, pl.ds(base, 8), :]
            o8 = jnp.zeros((8, d), F32)
            for j in range(8):
                slot = j % NSLOT
                wait(slot)
                u = jnp.concatenate([bufs[slot][s] for s in range(nc)], axis=1).astype(BF16)
                vv = jnp.concatenate([bufs[slot][nc + s] for s in range(nc)], axis=1).astype(BF16)
                act = lax.dot_general(x8, u, (((1,), (1,)), ((), ())), preferred_element_type=F32)
                w = jnp.where(rowi == j, g8 * _gelu(act), 0.0)
                o8 = o8 + jnp.dot(w.astype(BF16), vv, preferred_element_type=F32)
                if not last or half8 * 8 + j + NSLOT < TBLK:
                    issue(cur, base + j + NSLOT, slot)
                dep = ((lax.bitcast_convert_type(act[0:1], jnp.uint32) >> 16) >> 16).astype(jnp.int32)
                mi, ai, _, s_i = _sel_step(s_i, rowk, dep)
                mj, aj, _, s_j = _sel_step(s_j, rowk, dep)
                mc, _, hit, cand = _sel_step(cand, order2, dep)
                for k, val in (("vi", mi), ("ii", ai), ("vj", mj), ("ij", aj), ("sc", mc),
                               ("ex", jnp.max(jnp.where(hit, eid, -1), axis=0, keepdims=True))):
                    out[k].append(val)
            acc[pl.ds(base, 8), :] = o8
        prev = (m + PEER_HEADS - 1) % PEER_HEADS
        sc = jnp.concatenate(out["sc"], axis=0)
        e = jnp.exp(sc - sc[0:1])
        idst[pl.ds(prev * TOPK, TOPK), :] = jnp.concatenate(out["ex"], axis=0)
        gt[pl.ds(prev * TOPK, TOPK), :] = e / jnp.sum(e, axis=0, keepdims=True)
        vals_s[0] = jnp.concatenate(out["vi"], axis=0)
        idx_s[0] = jnp.concatenate(out["ii"], axis=0)
        vals_s[1] = jnp.concatenate(out["vj"], axis=0)
        idx_s[1] = jnp.concatenate(out["ij"], axis=0)

    def body(m, carry):
        block(m, False)
        return carry

    nblk = tk // TBLK
    lax.fori_loop(0, nblk - 1, body, 0)
    block(nblk - 1, True)
    ex, g = _second_stage(vals_s[0], idx_s[0], vals_s[1], idx_s[1])
    idst[pl.ds((PEER_HEADS - 1) * TOPK, TOPK), :] = ex
    gt[pl.ds((PEER_HEADS - 1) * TOPK, TOPK), :] = g
    publish(nxt)

    @pl.when(step + 1 < pl.num_programs(0))
    def _():
        for t in range(NSLOT):
            issue(nxt, t, t)

    o_ref[...] = x1_ref[...] + gt2_ref[0] * (_rms(acc[...]) * g3_ref[...])


def _peer_call(st, hx, x1, gt2, g3, uv, tokens_per_batch):
    n, d = hx.shape
    nl, nk, _ = st.shape
    assert nl == 2 * PEER_HEADS and nk == NKEYS and TK // TBLK == PEER_HEADS and NPAIR == 128
    nsteps = n // TK
    per_b = tokens_per_batch // TK
    tok_spec = pl.BlockSpec((TK, d), lambda i: (i, 0))
    return pl.pallas_call(
        _peer_kernel,
        grid=(nsteps,),
        in_specs=[pl.BlockSpec((nl, nk, TK), lambda i: (0, 0, 0)),
                  pl.BlockSpec((nl, nk, TK), lambda i: (0, 0, jnp.minimum(i + 1, nsteps - 1))),
                  tok_spec, tok_spec,
                  pl.BlockSpec((1, 1, d), lambda i: (i // per_b, 0, 0)),
                  _const_spec(g3.shape),
                  pl.BlockSpec(memory_space=pl.ANY)],
        out_specs=tok_spec,
        out_shape=jax.ShapeDtypeStruct((n, d), F32),
        scratch_shapes=[pltpu.VMEM((2 * d // 128, NPAIR, 128), F32) for _ in range(NSLOT)]
        + [pltpu.SemaphoreType.DMA((NSLOT,)),
           pltpu.SMEM((2 * TK, NPAIR), jnp.int32),
           pltpu.VMEM((TK, NPAIR), jnp.int32),
           pltpu.VMEM((2, TK, NPAIR), F32),
           pltpu.SemaphoreType.DMA(()),
           pltpu.VMEM((NPAIR, TK), jnp.int32),
           pltpu.VMEM((NPAIR, TK), F32),
           pltpu.VMEM((2, TOPK, TK), F32),
           pltpu.VMEM((2, TOPK, TK), jnp.int32),
           pltpu.VMEM((TK, d), F32)],
        compiler_params=_cparams(("arbitrary",)),
        name="peer",
    )(st, st, hx, x1, gt2, g3, uv)


def kernel(x, c, ctx, c_ctx, w_mod, b_mod, norm_gains, w_in, w_out, conv_w, tshift_mu, decay_w0, decay_w2, iclr_a0,
           iclr_a2, gate_w2, k_k, k_a, r_k, lnx_g, lnx_b, peer_wq, peer_subkeys, peer_u, peer_v):
    b, t, d = x.shape
    n_ctx = ctx.shape[1]
    assert w_mod.shape[0] == 1 and d == 2 * DR and t % TM == 0 and n_ctx == TM and b + 1 <= 8
    ng = norm_gains[0]

    cin = jnp.concatenate([c, c_ctx[None], jnp.zeros((8 - b - 1, d), F32)], axis=0)
    mod = _mod_call(cin, w_mod[0], b_mod[0][None])
    lat = mod[:b].reshape(b, 6, d)
    cm = mod[b].reshape(6, d)

    def both(j):
        return jnp.stack([jnp.broadcast_to(cm[j], (b, d)), lat[:, j]], axis=1)[:, :, None, :]

    zeros = jnp.zeros((HEAD, DR), F32)
    w2 = jnp.concatenate([jnp.concatenate([decay_w2[0, 0], zeros], axis=1),
                          jnp.concatenate([zeros, decay_w2[0, 1]], axis=1)], axis=0)
    a2 = jnp.concatenate([jnp.concatenate([iclr_a2[0, 0], zeros], axis=1),
                          jnp.concatenate([zeros, iclr_a2[0, 1]], axis=1)], axis=0)
    hidx = jnp.arange(DR) // HEAD
    gsum = (hidx[:, None] == hidx[None, :]).astype(BF16)

    feats = _prep_call(ctx, x, both(0), both(1), ng[0][None], w_in[0].astype(BF16), tshift_mu[0].reshape(6, DR),
                       k_k[0][None], k_a[0][None], r_k[0].reshape(1, DR), decay_w0[0].reshape(1, 2 * DR), w2,
                       iclr_a0[0].reshape(1, 2 * DR), a2.astype(BF16), gate_w2[0].astype(BF16), conv_w[0], gsum)
    r, kap, v, kdf, kdb, akf, akb, lwf, lwb, yconv, gate, bonus = feats

    yf, yb = _scan_call(r, kap, v, kdf, kdb, akf, akb, lwf, lwb, n_ctx)

    sk = peer_subkeys[0].reshape(2 * PEER_HEADS, NKEYS, -1).astype(BF16)
    x1, hx, st = _post_call(yf, yb, bonus, gate, yconv, x, lat[:, 2][:, None], lat[:, 3][:, None], lat[:, 4][:, None],
                            ng[1][None], ng[2][None], lnx_g[0][None], lnx_b[0][None], w_out[0].astype(BF16),
                            peer_wq[0].astype(BF16), sk, gsum, n_ctx)

    uv = jnp.concatenate([peer_u[0], peer_v[0]], axis=1).reshape(-1, 2 * d // 128, 128)
    out = _peer_call(st, hx.reshape(b * t, d), x1.reshape(b * t, d), lat[:, 5][:, None], ng[3][None], uv, t)
    return out.reshape(b, t, d)
```

```python
import functools
import math

import jax
import jax.numpy as jnp
from jax import lax
from jax.experimental import pallas as pl
from jax.experimental.pallas import tpu as pltpu

F32 = jnp.float32
BF16 = jnp.bfloat16
HI = lax.Precision.HIGHEST

NORM_EPS = 1e-6
LNX_EPS = 64e-5
HEAD = 64
NHEAD = 8
DR = HEAD * NHEAD
GRID_W = 64
CHUNK = 64
TOPK = 16
NKEYS = 128
PEER_HEADS = 8
NPAIR = PEER_HEADS * TOPK
TM = 256
TK = 128
SCAN_NB = 4
VMEM_LIMIT = 56 * 1024 * 1024


def _cparams(sem):
    return pltpu.CompilerParams(dimension_semantics=sem, vmem_limit_bytes=VMEM_LIMIT)


def _const_spec(shape):
    nd = len(shape)
    return pl.BlockSpec(shape, lambda *_: (0,) * nd)


def _rms(x):
    return x * lax.rsqrt(jnp.mean(x * x, axis=-1, keepdims=True) + NORM_EPS)


def _dot_ones(z, ones):
    hi = z.astype(BF16)
    lo = (z - hi.astype(F32)).astype(BF16)
    both = jnp.dot(jnp.concatenate([hi, lo], axis=0), ones, preferred_element_type=F32)
    return both[:z.shape[0]] + both[z.shape[0]:]


def _mod_kernel(c_ref, w_ref, b_ref, o_ref):
    c = c_ref[...]
    s = c * jax.nn.sigmoid(c)
    o_ref[...] = jnp.dot(s.astype(BF16), w_ref[...].astype(BF16), preferred_element_type=F32) + b_ref[...]


def _mod_call(cin, w, b):
    rows, d = cin.shape
    n = w.shape[1]
    tn = 1024
    return pl.pallas_call(
        _mod_kernel,
        grid=(n // tn,),
        in_specs=[_const_spec((rows, d)), pl.BlockSpec((d, tn), lambda j: (0, j)), pl.BlockSpec((1, tn), lambda j: (0, j))],
        out_specs=pl.BlockSpec((rows, tn), lambda j: (0, j)),
        out_shape=jax.ShapeDtypeStruct((rows, n), F32),
        compiler_params=_cparams(("arbitrary",)),
        name="mod",
    )(cin, w, b)


def _prep_kernel(nt, ctx_ref, xm_ref, xp_ref, xn_ref, sh_ref, sc_ref, g0_ref, win_ref, mu_ref, kk_ref, ka_ref, rk_ref,
                 w0_ref, w2_ref, a0_ref, a2_ref, gw2_ref, cw_ref, gsum_ref,
                 r_o, kap_o, v_o, kdf_o, kdb_o, akf_o, akb_o, lwf_o, lwb_o, yc_o, gate_o, bon_o):
    i = pl.program_id(1)
    g0 = g0_ref[...]
    sh = sh_ref[0, 0]
    sc = sc_ref[0, 0]

    def norm_mod(x):
        return (_rms(x) * g0 * (1.0 + sc) + sh).astype(BF16)

    hb = norm_mod(jnp.where(i == 0, ctx_ref[0], xm_ref[0]))
    halo = norm_mod(jnp.concatenate([xp_ref[0], xn_ref[0]], axis=0))

    p_conv = jnp.dot(hb, win_ref[:, 0:3 * DR], preferred_element_type=F32)
    p_rkv = jnp.dot(hb, win_ref[:, 3 * DR:6 * DR], preferred_element_type=F32)
    p_lo = jnp.dot(hb, win_ref[:, 6 * DR:], preferred_element_type=F32)
    p_halo = jnp.dot(halo, win_ref[:, 3 * DR:6 * DR], preferred_element_type=F32)

    has_prev = (i >= 2).astype(F32)
    has_next = jnp.logical_and(i >= 1, i <= nt - 2).astype(F32)
    first = p_halo[7:8] * has_prev
    last = p_halo[8:9] * has_next

    rows = lax.broadcasted_iota(jnp.int32, (TM, 1), 0)

    def prev_of(z, row0):
        return jnp.where(rows == 0, row0, pltpu.roll(z, 1, axis=0))

    def next_of(z, rowl):
        return jnp.where(rows == TM - 1, rowl, pltpu.roll(z, TM - 1, axis=0))

    mu = mu_ref[...]

    def tshift(j):
        z = p_rkv[:, j * DR:(j + 1) * DR]
        zp = prev_of(z, first[:, j * DR:(j + 1) * DR])
        zn = next_of(z, last[:, j * DR:(j + 1) * DR])
        return z + mu[j:j + 1] * (zp - z) + mu[3 + j:4 + j] * (zn - z)

    r = tshift(0)
    k = tshift(1)
    v = tshift(2)

    gsum = gsum_ref[...]

    kk = k * kk_ref[...]
    sums = _dot_ones(jnp.concatenate([kk * kk, r * k * rk_ref[...]], axis=0), gsum)
    kap = kk * lax.rsqrt(sums[:TM] + 1e-12)
    bonus = sums[TM:] * v

    lo_w = jnp.tanh(p_lo[:, 128:256])
    zw = jnp.dot(lo_w, w2_ref[...], precision=HI, preferred_element_type=F32) + w0_ref[...]
    lw = -math.exp(-0.5) * jax.nn.sigmoid(zw)
    za = jnp.dot(p_lo[:, 256:384].astype(BF16), a2_ref[...], preferred_element_type=F32) + a0_ref[...]
    a = jax.nn.sigmoid(za)
    ka = ka_ref[...]

    r_o[0] = r
    kap_o[0] = kap
    v_o[0] = v
    for d, (kd_o, ak_o, lw_o) in enumerate(((kdf_o, akf_o, lwf_o), (kdb_o, akb_o, lwb_o))):
        a_d = a[:, d * DR:(d + 1) * DR]
        kd_o[0] = k * (1.0 + (a_d - 1.0) * ka)
        ak_o[0] = a_d * kap
        lw_o[0] = lw[:, d * DR:(d + 1) * DR]

    gate_o[0] = jnp.dot(jax.nn.sigmoid(p_lo[:, 0:128]).astype(BF16), gw2_ref[...], preferred_element_type=F32)
    bon_o[0] = bonus

    z = p_conv[:, DR:2 * DR] * p_conv[:, 2 * DR:3 * DR]
    col = rows % GRID_W
    zp = jnp.where(col == 0, 0.0, pltpu.roll(z, 1, axis=0))
    zn = jnp.where(col == GRID_W - 1, 0.0, pltpu.roll(z, TM - 1, axis=0))
    cw = cw_ref[...]
    yc_o[0] = p_conv[:, 0:DR] * (cw[0:1] * zp + cw[1:2] * z + cw[2:3] * zn)


def _prep_call(ctx, x, sh_all, sc_all, g0, win, mu6, k_k, k_a, r_k, w0, w2, a0, a2, gw2, cw, gsum):
    b, t, d = x.shape
    tt = t + ctx.shape[1]
    nt = tt // TM
    nb8 = t // 8
    per8 = TM // 8
    ctx_spec = pl.BlockSpec((1, TM, d), lambda bb, i: (bb, 0, 0))
    row_spec = pl.BlockSpec((1, TM, d), lambda bb, i: (bb, jnp.maximum(i - 1, 0), 0))
    prev_spec = pl.BlockSpec((1, 8, d), lambda bb, i: (bb, jnp.maximum((i - 1) * per8 - 1, 0), 0))
    next_spec = pl.BlockSpec((1, 8, d), lambda bb, i: (bb, jnp.clip(i * per8, 0, nb8 - 1), 0))
    mod_spec = pl.BlockSpec((1, 1, 1, d), lambda bb, i: (bb, jnp.minimum(i, 1), 0, 0))
    consts = (g0, win, mu6, k_k, k_a, r_k, w0, w2, a0, a2, gw2, cw, gsum)
    out_spec = pl.BlockSpec((1, TM, DR), lambda bb, i: (bb, i, 0))
    out_sds = jax.ShapeDtypeStruct((b, tt, DR), F32)
    return pl.pallas_call(
        functools.partial(_prep_kernel, nt),
        grid=(b, nt),
        in_specs=[ctx_spec, row_spec, prev_spec, next_spec, mod_spec, mod_spec] + [_const_spec(z.shape) for z in consts],
        out_specs=[out_spec] * 12,
        out_shape=[out_sds] * 12,
        compiler_params=_cparams(("arbitrary", "arbitrary")),
        name="prep",
    )(ctx, x, x, x, sh_all, sc_all, *consts)


def _scan_chunk(r, kap, v, kd, ak, lw, h, rev, bmask, bmask_f, tri):
    c = CHUNK
    t_idx = lax.broadcasted_iota(jnp.int32, (c, DR), 0)
    s_idx = lax.broadcasted_iota(jnp.int32, (c, DR), 1) % c
    if rev:
        strict = s_idx > t_idx
        incl = s_idx >= t_idx
    else:
        strict = s_idx < t_idx
        incl = s_idx <= t_idx
    eye = (s_idx == t_idx).astype(F32)

    cum = jnp.dot(tri, lw, precision=HI, preferred_element_type=F32)
    yield
    tot = cum[0:1] if rev else cum[c - 1:c]
    e_in = jnp.exp(cum)
    e_out = jnp.exp(-cum)
    e_rest = jnp.exp(tot - cum)
    kt = kap * jnp.exp(cum - lw)
    bh = ak * e_out
    kh = kd * e_out
    rt = r * e_in
    kbar = kd * e_rest
    bbar = ak * e_rest

    def bd(z):
        return jnp.concatenate([z.astype(BF16)] * NHEAD, axis=0) * bmask

    def unbd(full):
        return (full * bmask_f).reshape(NHEAD, HEAD, DR).sum(axis=0)

    def mm(lhs, rhs):
        return jnp.dot(lhs.astype(BF16), rhs, preferred_element_type=F32)

    amat = lax.dot_general(jnp.concatenate([kt, rt], axis=0).astype(BF16),
                           jnp.concatenate([bd(bh), bd(kh)], axis=0),
                           (((1,), (1,)), ((), ())), preferred_element_type=F32)
    yield
    n0 = jnp.where(strict, -amat[:c, :DR], 0.0)
    akk = jnp.where(strict, amat[:c, DR:], 0.0)
    arb = jnp.where(incl, amat[c:, :DR], 0.0)
    ark = jnp.where(incl, amat[c:, DR:], 0.0)

    p = n0
    tinv = eye + n0
    bdv = bd(v)
    av = mm(jnp.concatenate([akk, ark], axis=0), bdv)
    p = mm(p, bd(p))
    yield
    for _ in range(int(math.log2(c)) - 2):
        out = mm(jnp.concatenate([p, tinv], axis=0), bd(p))
        yield
        tinv = tinv + out[c:]
        p = out[:c]
    tinv = tinv + mm(tinv, bd(p))
    yield
    wu = mm(tinv, jnp.concatenate([bd(kt), bd(av[:c])], axis=1))
    yield
    w = wu[:, :DR]
    uloc = wu[:, DR:]

    hw = mm(jnp.concatenate([rt, w], axis=0), bd(h))
    yield
    u = uloc + hw[c:]
    y = av[c:] + hw[:c] - mm(arb, bd(u))
    lhs_t = jnp.concatenate([kbar, -bbar], axis=0).astype(BF16)
    rhs_t = jnp.concatenate([v, u], axis=0).astype(BF16)
    full = lax.dot_general(lhs_t, rhs_t, (((0,), (0,)), ((), ())), preferred_element_type=F32)
    yield
    pct = jnp.exp(jnp.broadcast_to(tot, (128, DR))).T
    low = lax.broadcasted_iota(jnp.int32, (HEAD, 128), 1) < HEAD
    pcb = jnp.concatenate([jnp.where(low, pct[2 * q * HEAD:(2 * q + 1) * HEAD], pct[(2 * q + 1) * HEAD:(2 * q + 2) * HEAD])
                           for q in range(NHEAD // 2)], axis=1)
    h_new = pcb * h + unbd(full)
    return y, h_new


def _interleave(gens):
    results = [None] * len(gens)
    live = list(enumerate(gens))
    while live:
        still = []
        for i, g in live:
            try:
                next(g)
                still.append((i, g))
            except StopIteration as stop:
                results[i] = stop.value
        live = still
    return results


def _scan_kernel(rf, kapf, vf, kdf, akf, lwf, rb, kapb, vb, kdb, akb, lwb, bm_ref, bmf_ref, trif_ref, trib_ref,
                 yf_o, yb_o, hf_s, hb_s):
    @pl.when(pl.program_id(1) == 0)
    def _():
        hf_s[...] = jnp.zeros_like(hf_s)
        hb_s[...] = jnp.zeros_like(hb_s)

    bm = bm_ref[...]
    bmf = bmf_ref[...]
    chains = []
    for i in range(rf.shape[0]):
        chains.append((yf_o, hf_s, i, (rf[i], kapf[i], vf[i], kdf[i], akf[i], lwf[i], hf_s[i], False, bm, bmf, trif_ref[...])))
        chains.append((yb_o, hb_s, i, (rb[i], kapb[i], vb[i], kdb[i], akb[i], lwb[i], hb_s[i], True, bm, bmf, trib_ref[...])))
    results = _interleave([_scan_chunk(*args) for (_, _, _, args) in chains])
    for (y_o, h_s, i, _), (y, h_new) in zip(chains, results):
        y_o[i] = y
        h_s[i] = h_new


def _scan_call(r, kap, v, kdf, kdb, akf, akb, lwf, lwb, n_ctx):
    b, tt, _ = r.shape
    nch = tt // CHUNK
    ncc = n_ctx // CHUNK
    hidx = jnp.arange(DR) // HEAD
    bmask_f = (hidx[:, None] == hidx[None, :]).astype(F32)
    ii = jnp.arange(CHUNK)
    tri_f = (ii[None, :] <= ii[:, None]).astype(F32)
    tri_b = (ii[None, :] >= ii[:, None]).astype(F32)

    def bwd_chunk(s):
        return jnp.where(s < ncc, ncc - 1 - s, nch - 1 - (s - ncc))

    nb = SCAN_NB if b % SCAN_NB == 0 else 1
    fspec = pl.BlockSpec((nb, CHUNK, DR), lambda bb, s: (bb, s, 0))
    bspec = pl.BlockSpec((nb, CHUNK, DR), lambda bb, s: (bb, bwd_chunk(s), 0))
    consts = (bmask_f.astype(BF16), bmask_f, tri_f, tri_b)
    sds = jax.ShapeDtypeStruct((b, tt, DR), F32)
    return pl.pallas_call(
        _scan_kernel,
        grid=(b // nb, nch),
        in_specs=[fspec] * 6 + [bspec] * 6 + [_const_spec(z.shape) for z in consts],
        out_specs=[fspec, bspec],
        out_shape=[sds, sds],
        scratch_shapes=[pltpu.VMEM((nb, HEAD, DR), F32), pltpu.VMEM((nb, HEAD, DR), F32)],
        compiler_params=_cparams(("arbitrary", "arbitrary")),
        name="scan",
    )(r, kap, v, kdf, akf, lwf, r, kap, v, kdb, akb, lwb, *consts)


def _post_kernel(yf_ref, yb_ref, bon_ref, gate_ref, yc_ref, x_ref, gt1_ref, sh2_ref, sc2_ref, g1_ref, g2_ref,
                 lg_ref, lb_ref, wout_ref, wq_ref, sk_ref, gmean_ref, x1_o, hx_o, st_o):
    gmean = gmean_ref[...]

    def group_mean(z):
        return _dot_ones(z, gmean) * (1.0 / HEAD)

    y = yf_ref[0] + yb_ref[0]
    dlt = y - group_mean(y)
    yn = dlt * lax.rsqrt(group_mean(dlt * dlt) + LNX_EPS) * lg_ref[...] + lb_ref[...]
    y_rwkv = (yn + bon_ref[0]) * gate_ref[0]
    cat = jnp.concatenate([yc_ref[0], y_rwkv], axis=-1).astype(BF16)
    o = jnp.dot(cat, wout_ref[...], preferred_element_type=F32)
    x1 = x_ref[0] + gt1_ref[0] * (_rms(o) * g1_ref[...])
    x1_o[0] = x1
    hx = _rms(x1) * g2_ref[...] * (1.0 + sc2_ref[0]) + sh2_ref[0]
    hx_o[0] = hx
    q = jnp.dot(hx.astype(BF16), wq_ref[...], preferred_element_type=F32).astype(BF16)
    for l in range(2 * PEER_HEADS):
        ql = q[:, l * 128:(l + 1) * 128]
        st_o[l] = lax.dot_general(sk_ref[l], ql, (((1,), (1,)), ((), ())), preferred_element_type=F32)


def _post_call(yf, yb, bonus, gate, yconv, x, gt1, sh2, sc2, g1, g2, lnx_g, lnx_b, wout, wq, sk, gmean, n_ctx):
    b, t, d = x.shape
    nt = t // TM
    off = n_ctx // TM
    mix_spec = pl.BlockSpec((1, TM, DR), lambda bb, i: (bb, i + off, 0))
    x_spec = pl.BlockSpec((1, TM, d), lambda bb, i: (bb, i, 0))
    mod_spec = pl.BlockSpec((1, 1, d), lambda bb, i: (bb, 0, 0))
    consts = (g1, g2, lnx_g, lnx_b, wout, wq, sk, gmean)
    nl = 2 * PEER_HEADS
    return pl.pallas_call(
        _post_kernel,
        grid=(b, nt),
        in_specs=[mix_spec] * 5 + [x_spec] + [mod_spec] * 3 + [_const_spec(z.shape) for z in consts],
        out_specs=[x_spec, x_spec, pl.BlockSpec((nl, NKEYS, TM), lambda bb, i: (0, 0, bb * nt + i))],
        out_shape=[jax.ShapeDtypeStruct((b, t, d), F32), jax.ShapeDtypeStruct((b, t, d), F32),
                   jax.ShapeDtypeStruct((nl, NKEYS, b * t), F32)],
        compiler_params=_cparams(("arbitrary", "arbitrary")),
        name="post",
    )(yf, yb, bonus, gate, yconv, x, gt1, sh2, sc2, *consts)


def _sel_step(xv, order, dep=0):
    big = jnp.int32(1 << 30)
    order = order + dep
    m = jnp.max(xv, axis=0, keepdims=True)
    am = jnp.min(jnp.where(xv == m, order, big), axis=0, keepdims=True)
    hit = order == am
    return m, am, hit, jnp.where(hit, -jnp.inf, xv)


def _top16(xv, order):
    vals, picks, hits = [], [], []
    for _ in range(TOPK):
        m, am, hit, xv = _sel_step(xv, order)
        vals.append(m)
        picks.append(am)
        hits.append(hit)
    return vals, picks, hits


def _candidates(vi, ii, vj, ij):
    tk = vi.shape[-1]
    crow = lax.broadcasted_iota(jnp.int32, (10 * 8, tk), 0)
    ca = jnp.where(crow < 16, 0, jnp.where(crow < 72, 1 + (crow - 16) // 8, crow - 64))
    cb = jnp.where(crow < 16, crow, jnp.where(crow < 72, (crow - 16) % 8, 0))
    stair = (ca + 1) * (cb + 1) <= TOPK
    order2 = ca * TOPK + cb
    cand = jnp.concatenate([vi[0:1] + vj] + [vi[a:a + 1] + vj[0:8] for a in range(1, 8)] + [vi[8:16] + vj[0:1]],
                           axis=0)
    eid = jnp.concatenate([ii[0:1] * NKEYS + ij] + [ii[a:a + 1] * NKEYS + ij[0:8] for a in range(1, 8)]
                          + [ii[8:16] * NKEYS + ij[0:1]], axis=0)
    return jnp.where(stair, cand, -jnp.inf), eid, order2


def _second_stage(vi, ii, vj, ij):
    cand, eid, order2 = _candidates(vi, ii, vj, ij)
    vals, _, hits = _top16(cand, order2)
    sc = jnp.concatenate(vals, axis=0)
    ex = jnp.concatenate([jnp.max(jnp.where(s, eid, -1), axis=0, keepdims=True) for s in hits], axis=0)
    e = jnp.exp(sc - sc[0:1])
    return ex, e / jnp.sum(e, axis=0, keepdims=True)


def _topk_head(s_i, s_j):
    rowi = lax.broadcasted_iota(jnp.int32, s_i.shape, 0)
    tops = []
    for s in (s_i, s_j):
        vals, picks, _ = _top16(s, rowi)
        tops.append((jnp.concatenate(vals, axis=0), jnp.concatenate(picks, axis=0)))
    (vi, ii), (vj, ij) = tops
    return _second_stage(vi, ii, vj, ij)


def _gelu(x):
    return 0.5 * x * (1.0 + lax.erf(x * (1.0 / math.sqrt(2.0))))


NSLOT = 4
TBLK = 2 * 8


def _peer_kernel(st0_ref, stn_ref, hx_ref, x1_ref, gt2_ref, g3_ref, uv_hbm, o_ref, *scratch):
    bufs = scratch[:NSLOT]
    sem, ids_sm, ids_v, g_s, sem_ids, idst, gt, vals_s, idx_s, acc = scratch[NSLOT:]
    tk, d = hx_ref.shape
    nc = d // 128
    step = pl.program_id(0)
    cur = step % 2
    nxt = 1 - cur

    def issue(half, t, slot):
        for p in range(NPAIR):
            e = ids_sm[half * tk + t, p]
            pltpu.make_async_copy(uv_hbm.at[e], bufs[slot].at[:, p, :], sem.at[slot]).start(priority=p % 2)

    def wait(slot):
        pltpu.make_async_copy(bufs[slot], bufs[slot], sem.at[slot]).wait()

    def publish(half):
        g_s[half] = gt[...].T
        ids_v[...] = idst[...].T
        cp = pltpu.make_async_copy(ids_v, ids_sm.at[pl.ds(half * tk, tk)], sem_ids)
        cp.start()
        cp.wait()

    def retrieve(st_ref, h):
        ex, g = _topk_head(st_ref[2 * h], st_ref[2 * h + 1])
        idst[pl.ds(h * TOPK, TOPK), :] = ex
        gt[pl.ds(h * TOPK, TOPK), :] = g

    @pl.when(step == 0)
    def _():
        for h in range(PEER_HEADS):
            retrieve(st0_ref, h)
        publish(0)
        vals_s[...] = jnp.zeros_like(vals_s)
        idx_s[...] = jnp.zeros_like(idx_s)
        for t in range(NSLOT):
            issue(0, t, t)

    rowi = lax.broadcasted_iota(jnp.int32, (8, NPAIR), 0)

    rowk = lax.broadcasted_iota(jnp.int32, (NKEYS, tk), 0)
    nblk = tk // TBLK

    def expert_token(slot, j, x8, g8, o8):
        wait(slot)
        u = jnp.concatenate([bufs[slot][s] for s in range(nc)], axis=1).astype(BF16)
        vv = jnp.concatenate([bufs[slot][nc + s] for s in range(nc)], axis=1).astype(BF16)
        act = lax.dot_general(x8, u, (((1,), (1,)), ((), ())), preferred_element_type=F32)
        w = jnp.where(rowi == j, g8 * _gelu(act), 0.0)
        dep = ((lax.bitcast_convert_type(act[0:1], jnp.uint32) >> 16) >> 16).astype(jnp.int32)
        return o8 + jnp.dot(w.astype(BF16), vv, preferred_element_type=F32), dep

    def first_stage_pick(state, out, dep):
        mi, ai, _, s_i = _sel_step(state[0], rowk, dep)
        mj, aj, _, s_j = _sel_step(state[1], rowk, dep)
        for k, val in (("vi", mi), ("ii", ai), ("vj", mj), ("ij", aj)):
            out[k].append(val)
        return s_i, s_j

    def second_stage_pick(cand, eid, order2, out, dep):
        mc, _, hit, cand = _sel_step(cand, order2, dep)
        out["sc"].append(mc)
        out["ex"].append(jnp.max(jnp.where(hit, eid, -1), axis=0, keepdims=True))
        return cand

    def store_head(h, out):
        sc = jnp.concatenate(out["sc"], axis=0)
        e = jnp.exp(sc - sc[0:1])
        idst[pl.ds(h * TOPK, TOPK), :] = jnp.concatenate(out["ex"], axis=0)
        gt[pl.ds(h * TOPK, TOPK), :] = e / jnp.sum(e, axis=0, keepdims=True)

    def block(m):
        lists = (stn_ref[2 * m], stn_ref[2 * m + 1])
        cand, eid, order2 = _candidates(vals_s[0], idx_s[0], vals_s[1], idx_s[1])
        first = {k: [] for k in ("vi", "ii", "vj", "ij")}
        second = {"sc": [], "ex": []}
        for half8 in range(TBLK // 8):
            base = pl.multiple_of(m * TBLK + half8 * 8, 8)
            x8 = hx_ref[pl.ds(base, 8), :].astype(BF16)
            g8 = g_s[cur, pl.ds(base, 8), :]
            o8 = jnp.zeros((8, d), F32)
            for j in range(8):
                slot = j % NSLOT
                o8, dep = expert_token(slot, j, x8, g8, o8)
                issue(cur, base + j + NSLOT, slot)
                lists = first_stage_pick(lists, first, dep)
                cand = second_stage_pick(cand, eid, order2, second, dep)
            acc[pl.ds(base, 8), :] = o8
        store_head((m + PEER_HEADS - 1) % PEER_HEADS, second)
        vals_s[0] = jnp.concatenate(first["vi"], axis=0)
        idx_s[0] = jnp.concatenate(first["ii"], axis=0)
        vals_s[1] = jnp.concatenate(first["vj"], axis=0)
        idx_s[1] = jnp.concatenate(first["ij"], axis=0)

    def last_block():
        m = nblk - 1
        quick = (3, 3, 3, 3, 2, 2)
        lists = (stn_ref[2 * m], stn_ref[2 * m + 1])
        cand_p, eid_p, order2 = _candidates(vals_s[0], idx_s[0], vals_s[1], idx_s[1])
        first = {k: [] for k in ("vi", "ii", "vj", "ij")}
        second_p = {"sc": [], "ex": []}
        second_m = {"sc": [], "ex": []}
        cand_m = eid_m = None
        for q in range(TBLK):
            half8, j = divmod(q, 8)
            base = m * TBLK + half8 * 8
            if j == 0:
                x8 = hx_ref[pl.ds(base, 8), :].astype(BF16)
                g8 = g_s[cur, pl.ds(base, 8), :]
                o8 = jnp.zeros((8, d), F32)
            slot = j % NSLOT
            o8, dep = expert_token(slot, j, x8, g8, o8)
            if q + NSLOT < TBLK:
                issue(cur, base + j + NSLOT, slot)
            else:
                issue(nxt, q + NSLOT - TBLK, slot)
            if q < len(quick):
                for _ in range(quick[q]):
                    lists = first_stage_pick(lists, first, dep)
                    cand_p = second_stage_pick(cand_p, eid_p, order2, second_p, dep)
            elif q < 2 * len(quick):
                if q == len(quick):
                    cand_m, eid_m, _ = _candidates(*(jnp.concatenate(first[k], axis=0) for k in ("vi", "ii", "vj", "ij")))
                for _ in range(quick[q - len(quick)]):
                    cand_m = second_stage_pick(cand_m, eid_m, order2, second_m, dep)
            if q == 2 * len(quick) - 1:
                store_head(m - 1, second_p)
                store_head(m, second_m)
                publish(nxt)
            if j == 7:
                acc[pl.ds(base, 8), :] = o8

    def body(m, carry):
        block(m)
        return carry

    lax.fori_loop(0, nblk - 1, body, 0)
    last_block()

    @pl.when(step + 1 == pl.num_programs(0))
    def _():
        for t in range(NSLOT):
            wait(t)

    o_ref[...] = x1_ref[...] + gt2_ref[0] * (_rms(acc[...]) * g3_ref[...])


def _peer_call(st, hx, x1, gt2, g3, uv, tokens_per_batch):
    n, d = hx.shape
    nl, nk, _ = st.shape
    assert nl == 2 * PEER_HEADS and nk == NKEYS and TK // TBLK == PEER_HEADS and NPAIR == 128
    nsteps = n // TK
    per_b = tokens_per_batch // TK
    tok_spec = pl.BlockSpec((TK, d), lambda i: (i, 0))
    return pl.pallas_call(
        _peer_kernel,
        grid=(nsteps,),
        in_specs=[pl.BlockSpec((nl, nk, TK), lambda i: (0, 0, 0)),
                  pl.BlockSpec((nl, nk, TK), lambda i: (0, 0, jnp.minimum(i + 1, nsteps - 1))),
                  tok_spec, tok_spec,
                  pl.BlockSpec((1, 1, d), lambda i: (i // per_b, 0, 0)),
                  _const_spec(g3.shape),
                  pl.BlockSpec(memory_space=pl.ANY)],
        out_specs=tok_spec,
        out_shape=jax.ShapeDtypeStruct((n, d), F32),
        scratch_shapes=[pltpu.VMEM((2 * d // 128, NPAIR, 128), F32) for _ in range(NSLOT)]
        + [pltpu.SemaphoreType.DMA((NSLOT,)),
           pltpu.SMEM((2 * TK, NPAIR), jnp.int32),
           pltpu.VMEM((TK, NPAIR), jnp.int32),
           pltpu.VMEM((2, TK, NPAIR), F32),
           pltpu.SemaphoreType.DMA(()),
           pltpu.VMEM((NPAIR, TK), jnp.int32),
           pltpu.VMEM((NPAIR, TK), F32),
           pltpu.VMEM((2, TOPK, TK), F32),
           pltpu.VMEM((2, TOPK, TK), jnp.int32),
           pltpu.VMEM((TK, d), F32)],
        compiler_params=_cparams(("arbitrary",)),
        name="peer",
    )(st, st, hx, x1, gt2, g3, uv)


def kernel(x, c, ctx, c_ctx, w_mod, b_mod, norm_gains, w_in, w_out, conv_w, tshift_mu, decay_w0, decay_w2, iclr_a0,
           iclr_a2, gate_w2, k_k, k_a, r_k, lnx_g, lnx_b, peer_wq, peer_subkeys, peer_u, peer_v):
    b, t, d = x.shape
    n_ctx = ctx.shape[1]
    assert w_mod.shape[0] == 1 and d == 2 * DR and t % TM == 0 and n_ctx == TM and b + 1 <= 8
    ng = norm_gains[0]

    cin = jnp.concatenate([c, c_ctx[None], jnp.zeros((8 - b - 1, d), F32)], axis=0)
    mod = _mod_call(cin, w_mod[0], b_mod[0][None])
    lat = mod[:b].reshape(b, 6, d)
    cm = mod[b].reshape(6, d)

    def both(j):
        return jnp.stack([jnp.broadcast_to(cm[j], (b, d)), lat[:, j]], axis=1)[:, :, None, :]

    zeros = jnp.zeros((HEAD, DR), F32)
    w2 = jnp.concatenate([jnp.concatenate([decay_w2[0, 0], zeros], axis=1),
                          jnp.concatenate([zeros, decay_w2[0, 1]], axis=1)], axis=0)
    a2 = jnp.concatenate([jnp.concatenate([iclr_a2[0, 0], zeros], axis=1),
                          jnp.concatenate([zeros, iclr_a2[0, 1]], axis=1)], axis=0)
    hidx = jnp.arange(DR) // HEAD
    gsum = (hidx[:, None] == hidx[None, :]).astype(BF16)

    feats = _prep_call(ctx, x, both(0), both(1), ng[0][None], w_in[0].astype(BF16), tshift_mu[0].reshape(6, DR),
                       k_k[0][None], k_a[0][None], r_k[0].reshape(1, DR), decay_w0[0].reshape(1, 2 * DR), w2,
                       iclr_a0[0].reshape(1, 2 * DR), a2.astype(BF16), gate_w2[0].astype(BF16), conv_w[0], gsum)
    r, kap, v, kdf, kdb, akf, akb, lwf, lwb, yconv, gate, bonus = feats

    yf, yb = _scan_call(r, kap, v, kdf, kdb, akf, akb, lwf, lwb, n_ctx)

    sk = peer_subkeys[0].reshape(2 * PEER_HEADS, NKEYS, -1).astype(BF16)
    x1, hx, st = _post_call(yf, yb, bonus, gate, yconv, x, lat[:, 2][:, None], lat[:, 3][:, None], lat[:, 4][:, None],
                            ng[1][None], ng[2][None], lnx_g[0][None], lnx_b[0][None], w_out[0].astype(BF16),
                            peer_wq[0].astype(BF16), sk, gsum, n_ctx)

    uv = jnp.concatenate([peer_u[0], peer_v[0]], axis=1).reshape(-1, 2 * d // 128, 128)
    out = _peer_call(st, hx.reshape(b * t, d), x1.reshape(b * t, d), lat[:, 5][:, None], ng[3][None], uv, t)
    return out.reshape(b, t, d)
```

```python
import functools
import math

import jax
import jax.numpy as jnp
from jax import lax
from jax.experimental import pallas as pl
from jax.experimental.pallas import tpu as pltpu

F32 = jnp.float32
BF16 = jnp.bfloat16
HI = lax.Precision.HIGHEST

NORM_EPS = 1e-6
LNX_EPS = 64e-5
HEAD = 64
NHEAD = 8
DR = HEAD * NHEAD
GRID_W = 64
CHUNK = 64
TOPK = 16
NKEYS = 128
PEER_HEADS = 8
NPAIR = PEER_HEADS * TOPK
TM = 256
TK = 128
SCAN_NB = 4
VMEM_LIMIT = 56 * 1024 * 1024


def _cparams(sem):
    return pltpu.CompilerParams(dimension_semantics=sem, vmem_limit_bytes=VMEM_LIMIT)


def _const_spec(shape):
    nd = len(shape)
    return pl.BlockSpec(shape, lambda *_: (0,) * nd)


def _rms(x):
    return x * lax.rsqrt(jnp.mean(x * x, axis=-1, keepdims=True) + NORM_EPS)


def _dot_ones(z, ones):
    hi = z.astype(BF16)
    lo = (z - hi.astype(F32)).astype(BF16)
    both = jnp.dot(jnp.concatenate([hi, lo], axis=0), ones, preferred_element_type=F32)
    return both[:z.shape[0]] + both[z.shape[0]:]


def _mod_kernel(c_ref, w_ref, b_ref, o_ref):
    c = c_ref[...]
    s = c * jax.nn.sigmoid(c)
    o_ref[...] = jnp.dot(s.astype(BF16), w_ref[...].astype(BF16), preferred_element_type=F32) + b_ref[...]


def _mod_call(cin, w, b):
    rows, d = cin.shape
    n = w.shape[1]
    tn = 1024
    return pl.pallas_call(
        _mod_kernel,
        grid=(n // tn,),
        in_specs=[_const_spec((rows, d)), pl.BlockSpec((d, tn), lambda j: (0, j)), pl.BlockSpec((1, tn), lambda j: (0, j))],
        out_specs=pl.BlockSpec((rows, tn), lambda j: (0, j)),
        out_shape=jax.ShapeDtypeStruct((rows, n), F32),
        compiler_params=_cparams(("arbitrary",)),
        name="mod",
    )(cin, w, b)


def _prep_kernel(nt, ctx_ref, xm_ref, xp_ref, xn_ref, sh_ref, sc_ref, g0_ref, win_ref, mu_ref, kk_ref, ka_ref, rk_ref,
                 w0_ref, w2_ref, a0_ref, a2_ref, gw2_ref, cw_ref, gsum_ref,
                 r_o, kap_o, v_o, kdf_o, kdb_o, akf_o, akb_o, lwf_o, lwb_o, yc_o, gate_o, bon_o):
    i = pl.program_id(1)
    g0 = g0_ref[...]
    sh = sh_ref[0, 0]
    sc = sc_ref[0, 0]

    def norm_mod(x):
        return (_rms(x) * g0 * (1.0 + sc) + sh).astype(BF16)

    hb = norm_mod(jnp.where(i == 0, ctx_ref[0], xm_ref[0]))
    halo = norm_mod(jnp.concatenate([xp_ref[0], xn_ref[0]], axis=0))

    p_conv = jnp.dot(hb, win_ref[:, 0:3 * DR], preferred_element_type=F32)
    p_rkv = jnp.dot(hb, win_ref[:, 3 * DR:6 * DR], preferred_element_type=F32)
    p_lo = jnp.dot(hb, win_ref[:, 6 * DR:], preferred_element_type=F32)
    p_halo = jnp.dot(halo, win_ref[:, 3 * DR:6 * DR], preferred_element_type=F32)

    has_prev = (i >= 2).astype(F32)
    has_next = jnp.logical_and(i >= 1, i <= nt - 2).astype(F32)
    first = p_halo[7:8] * has_prev
    last = p_halo[8:9] * has_next

    rows = lax.broadcasted_iota(jnp.int32, (TM, 1), 0)

    def prev_of(z, row0):
        return jnp.where(rows == 0, row0, pltpu.roll(z, 1, axis=0))

    def next_of(z, rowl):
        return jnp.where(rows == TM - 1, rowl, pltpu.roll(z, TM - 1, axis=0))

    mu = mu_ref[...]

    def tshift(j):
        z = p_rkv[:, j * DR:(j + 1) * DR]
        zp = prev_of(z, first[:, j * DR:(j + 1) * DR])
        zn = next_of(z, last[:, j * DR:(j + 1) * DR])
        return z + mu[j:j + 1] * (zp - z) + mu[3 + j:4 + j] * (zn - z)

    r = tshift(0)
    k = tshift(1)
    v = tshift(2)

    gsum = gsum_ref[...]

    kk = k * kk_ref[...]
    sums = _dot_ones(jnp.concatenate([kk * kk, r * k * rk_ref[...]], axis=0), gsum)
    kap = kk * lax.rsqrt(sums[:TM] + 1e-12)
    bonus = sums[TM:] * v

    lo_w = jnp.tanh(p_lo[:, 128:256])
    zw = jnp.dot(lo_w, w2_ref[...], precision=HI, preferred_element_type=F32) + w0_ref[...]
    lw = -math.exp(-0.5) * jax.nn.sigmoid(zw)
    za = jnp.dot(p_lo[:, 256:384].astype(BF16), a2_ref[...], preferred_element_type=F32) + a0_ref[...]
    a = jax.nn.sigmoid(za)
    ka = ka_ref[...]

    r_o[0] = r
    kap_o[0] = kap
    v_o[0] = v
    for d, (kd_o, ak_o, lw_o) in enumerate(((kdf_o, akf_o, lwf_o), (kdb_o, akb_o, lwb_o))):
        a_d = a[:, d * DR:(d + 1) * DR]
        kd_o[0] = k * (1.0 + (a_d - 1.0) * ka)
        ak_o[0] = a_d * kap
        lw_o[0] = lw[:, d * DR:(d + 1) * DR]

    gate_o[0] = jnp.dot(jax.nn.sigmoid(p_lo[:, 0:128]).astype(BF16), gw2_ref[...], preferred_element_type=F32)
    bon_o[0] = bonus

    z = p_conv[:, DR:2 * DR] * p_conv[:, 2 * DR:3 * DR]
    col = rows % GRID_W
    zp = jnp.where(col == 0, 0.0, pltpu.roll(z, 1, axis=0))
    zn = jnp.where(col == GRID_W - 1, 0.0, pltpu.roll(z, TM - 1, axis=0))
    cw = cw_ref[...]
    yc_o[0] = p_conv[:, 0:DR] * (cw[0:1] * zp + cw[1:2] * z + cw[2:3] * zn)


def _prep_call(ctx, x, sh_all, sc_all, g0, win, mu6, k_k, k_a, r_k, w0, w2, a0, a2, gw2, cw, gsum):
    b, t, d = x.shape
    tt = t + ctx.shape[1]
    nt = tt // TM
    nb8 = t // 8
    per8 = TM // 8
    ctx_spec = pl.BlockSpec((1, TM, d), lambda bb, i: (bb, 0, 0))
    row_spec = pl.BlockSpec((1, TM, d), lambda bb, i: (bb, jnp.maximum(i - 1, 0), 0))
    prev_spec = pl.BlockSpec((1, 8, d), lambda bb, i: (bb, jnp.maximum((i - 1) * per8 - 1, 0), 0))
    next_spec = pl.BlockSpec((1, 8, d), lambda bb, i: (bb, jnp.clip(i * per8, 0, nb8 - 1), 0))
    mod_spec = pl.BlockSpec((1, 1, 1, d), lambda bb, i: (bb, jnp.minimum(i, 1), 0, 0))
    consts = (g0, win, mu6, k_k, k_a, r_k, w0, w2, a0, a2, gw2, cw, gsum)
    out_spec = pl.BlockSpec((1, TM, DR), lambda bb, i: (bb, i, 0))
    out_sds = jax.ShapeDtypeStruct((b, tt, DR), F32)
    return pl.pallas_call(
        functools.partial(_prep_kernel, nt),
        grid=(b, nt),
        in_specs=[ctx_spec, row_spec, prev_spec, next_spec, mod_spec, mod_spec] + [_const_spec(z.shape) for z in consts],
        out_specs=[out_spec] * 12,
        out_shape=[out_sds] * 12,
        compiler_params=_cparams(("arbitrary", "arbitrary")),
        name="prep",
    )(ctx, x, x, x, sh_all, sc_all, *consts)


def _scan_chunk(r, kap, v, kd, ak, lw, h, rev, bmask, bmask_f, tri):
    c = CHUNK
    t_idx = lax.broadcasted_iota(jnp.int32, (c, DR), 0)
    s_idx = lax.broadcasted_iota(jnp.int32, (c, DR), 1) % c
    if rev:
        strict = s_idx > t_idx
        incl = s_idx >= t_idx
    else:
        strict = s_idx < t_idx
        incl = s_idx <= t_idx
    eye = (s_idx == t_idx).astype(F32)

    cum = jnp.dot(tri, lw, precision=HI, preferred_element_type=F32)
    yield
    tot = cum[0:1] if rev else cum[c - 1:c]
    e_in = jnp.exp(cum)
    e_out = jnp.exp(-cum)
    e_rest = jnp.exp(tot - cum)
    kt = kap * jnp.exp(cum - lw)
    bh = ak * e_out
    kh = kd * e_out
    rt = r * e_in
    kbar = kd * e_rest
    bbar = ak * e_rest

    def bd(z):
        return jnp.concatenate([z.astype(BF16)] * NHEAD, axis=0) * bmask

    def unbd(full):
        return (full * bmask_f).reshape(NHEAD, HEAD, DR).sum(axis=0)

    def mm(lhs, rhs):
        return jnp.dot(lhs.astype(BF16), rhs, preferred_element_type=F32)

    amat = lax.dot_general(jnp.concatenate([kt, rt], axis=0).astype(BF16),
                           jnp.concatenate([bd(bh), bd(kh)], axis=0),
                           (((1,), (1,)), ((), ())), preferred_element_type=F32)
    yield
    n0 = jnp.where(strict, -amat[:c, :DR], 0.0)
    akk = jnp.where(strict, amat[:c, DR:], 0.0)
    arb = jnp.where(incl, amat[c:, :DR], 0.0)
    ark = jnp.where(incl, amat[c:, DR:], 0.0)

    def same_block(size):
        return (t_idx // size) == (s_idx // size)

    base = 8
    p = jnp.where(same_block(base), n0, 0.0)
    tinv = eye + p
    bdv = bd(v)
    av = mm(jnp.concatenate([akk, ark], axis=0), bdv)
    p = mm(p, bd(p))
    yield
    out = mm(jnp.concatenate([p, tinv], axis=0), bd(p))
    yield
    tinv = tinv + out[c:]
    tinv = tinv + mm(tinv, bd(out[:c]))
    yield
    size = base
    while size < c:
        off = jnp.where(jnp.logical_and(same_block(2 * size), jnp.logical_not(same_block(size))), -n0, 0.0)
        z = mm(tinv, bd(off))
        yield
        tinv = tinv - mm(z, bd(tinv))
        yield
        size *= 2
    wu = mm(tinv, jnp.concatenate([bd(kt), bd(av[:c])], axis=1))
    yield
    w = wu[:, :DR]
    uloc = wu[:, DR:]

    hw = mm(jnp.concatenate([rt, w], axis=0), bd(h))
    yield
    u = uloc + hw[c:]
    y = av[c:] + hw[:c] - mm(arb, bd(u))
    lhs_t = jnp.concatenate([kbar, -bbar], axis=0).astype(BF16)
    rhs_t = jnp.concatenate([v, u], axis=0).astype(BF16)
    full = lax.dot_general(lhs_t, rhs_t, (((0,), (0,)), ((), ())), preferred_element_type=F32)
    yield
    pct = jnp.exp(jnp.broadcast_to(tot, (128, DR))).T
    low = lax.broadcasted_iota(jnp.int32, (HEAD, 128), 1) < HEAD
    pcb = jnp.concatenate([jnp.where(low, pct[2 * q * HEAD:(2 * q + 1) * HEAD], pct[(2 * q + 1) * HEAD:(2 * q + 2) * HEAD])
                           for q in range(NHEAD // 2)], axis=1)
    h_new = pcb * h + unbd(full)
    return y, h_new


def _interleave(gens):
    results = [None] * len(gens)
    live = list(enumerate(gens))
    while live:
        still = []
        for i, g in live:
            try:
                next(g)
                still.append((i, g))
            except StopIteration as stop:
                results[i] = stop.value
        live = still
    return results


def _scan_kernel(rf, kapf, vf, kdf, akf, lwf, rb, kapb, vb, kdb, akb, lwb, bm_ref, bmf_ref, trif_ref, trib_ref,
                 yf_o, yb_o, hf_s, hb_s):
    @pl.when(pl.program_id(1) == 0)
    def _():
        hf_s[...] = jnp.zeros_like(hf_s)
        hb_s[...] = jnp.zeros_like(hb_s)

    bm = bm_ref[...]
    bmf = bmf_ref[...]
    chains = []
    for i in range(rf.shape[0]):
        chains.append((yf_o, hf_s, i, (rf[i], kapf[i], vf[i], kdf[i], akf[i], lwf[i], hf_s[i], False, bm, bmf, trif_ref[...])))
        chains.append((yb_o, hb_s, i, (rb[i], kapb[i], vb[i], kdb[i], akb[i], lwb[i], hb_s[i], True, bm, bmf, trib_ref[...])))
    results = _interleave([_scan_chunk(*args) for (_, _, _, args) in chains])
    for (y_o, h_s, i, _), (y, h_new) in zip(chains, results):
        y_o[i] = y
        h_s[i] = h_new


def _scan_call(r, kap, v, kdf, kdb, akf, akb, lwf, lwb, n_ctx):
    b, tt, _ = r.shape
    nch = tt // CHUNK
    ncc = n_ctx // CHUNK
    hidx = jnp.arange(DR) // HEAD
    bmask_f = (hidx[:, None] == hidx[None, :]).astype(F32)
    ii = jnp.arange(CHUNK)
    tri_f = (ii[None, :] <= ii[:, None]).astype(F32)
    tri_b = (ii[None, :] >= ii[:, None]).astype(F32)

    def bwd_chunk(s):
        return jnp.where(s < ncc, ncc - 1 - s, nch - 1 - (s - ncc))

    nb = SCAN_NB if b % SCAN_NB == 0 else 1
    fspec = pl.BlockSpec((nb, CHUNK, DR), lambda bb, s: (bb, s, 0))
    bspec = pl.BlockSpec((nb, CHUNK, DR), lambda bb, s: (bb, bwd_chunk(s), 0))
    consts = (bmask_f.astype(BF16), bmask_f, tri_f, tri_b)
    sds = jax.ShapeDtypeStruct((b, tt, DR), F32)
    return pl.pallas_call(
        _scan_kernel,
        grid=(b // nb, nch),
        in_specs=[fspec] * 6 + [bspec] * 6 + [_const_spec(z.shape) for z in consts],
        out_specs=[fspec, bspec],
        out_shape=[sds, sds],
        scratch_shapes=[pltpu.VMEM((nb, HEAD, DR), F32), pltpu.VMEM((nb, HEAD, DR), F32)],
        compiler_params=_cparams(("arbitrary", "arbitrary")),
        name="scan",
    )(r, kap, v, kdf, akf, lwf, r, kap, v, kdb, akb, lwb, *consts)


def _post_kernel(yf_ref, yb_ref, bon_ref, gate_ref, yc_ref, x_ref, gt1_ref, sh2_ref, sc2_ref, g1_ref, g2_ref,
                 lg_ref, lb_ref, wout_ref, wq_ref, sk_ref, gmean_ref, x1_o, hx_o, st_o):
    gmean = gmean_ref[...]

    def group_mean(z):
        return _dot_ones(z, gmean) * (1.0 / HEAD)

    y = yf_ref[0] + yb_ref[0]
    dlt = y - group_mean(y)
    yn = dlt * lax.rsqrt(group_mean(dlt * dlt) + LNX_EPS) * lg_ref[...] + lb_ref[...]
    y_rwkv = (yn + bon_ref[0]) * gate_ref[0]
    cat = jnp.concatenate([yc_ref[0], y_rwkv], axis=-1).astype(BF16)
    o = jnp.dot(cat, wout_ref[...], preferred_element_type=F32)
    x1 = x_ref[0] + gt1_ref[0] * (_rms(o) * g1_ref[...])
    x1_o[0] = x1
    hx = _rms(x1) * g2_ref[...] * (1.0 + sc2_ref[0]) + sh2_ref[0]
    hx_o[0] = hx
    q = jnp.dot(hx.astype(BF16), wq_ref[...], preferred_element_type=F32).astype(BF16)
    for l in range(2 * PEER_HEADS):
        ql = q[:, l * 128:(l + 1) * 128]
        st_o[l] = lax.dot_general(sk_ref[l], ql, (((1,), (1,)), ((), ())), preferred_element_type=F32)


def _post_call(yf, yb, bonus, gate, yconv, x, gt1, sh2, sc2, g1, g2, lnx_g, lnx_b, wout, wq, sk, gmean, n_ctx):
    b, t, d = x.shape
    nt = t // TM
    off = n_ctx // TM
    mix_spec = pl.BlockSpec((1, TM, DR), lambda bb, i: (bb, i + off, 0))
    x_spec = pl.BlockSpec((1, TM, d), lambda bb, i: (bb, i, 0))
    mod_spec = pl.BlockSpec((1, 1, d), lambda bb, i: (bb, 0, 0))
    consts = (g1, g2, lnx_g, lnx_b, wout, wq, sk, gmean)
    nl = 2 * PEER_HEADS
    return pl.pallas_call(
        _post_kernel,
        grid=(b, nt),
        in_specs=[mix_spec] * 5 + [x_spec] + [mod_spec] * 3 + [_const_spec(z.shape) for z in consts],
        out_specs=[x_spec, x_spec, pl.BlockSpec((nl, NKEYS, TM), lambda bb, i: (0, 0, bb * nt + i))],
        out_shape=[jax.ShapeDtypeStruct((b, t, d), F32), jax.ShapeDtypeStruct((b, t, d), F32),
                   jax.ShapeDtypeStruct((nl, NKEYS, b * t), F32)],
        compiler_params=_cparams(("arbitrary", "arbitrary")),
        name="post",
    )(yf, yb, bonus, gate, yconv, x, gt1, sh2, sc2, *consts)


def _sel_step(xv, order, dep=0):
    big = jnp.int32(1 << 30)
    order = order + dep
    m = jnp.max(xv, axis=0, keepdims=True)
    am = jnp.min(jnp.where(xv == m, order, big), axis=0, keepdims=True)
    hit = order == am
    return m, am, hit, jnp.where(hit, -jnp.inf, xv)


def _top16(xv, order):
    vals, picks, hits = [], [], []
    for _ in range(TOPK):
        m, am, hit, xv = _sel_step(xv, order)
        vals.append(m)
        picks.append(am)
        hits.append(hit)
    return vals, picks, hits


def _candidates(vi, ii, vj, ij):
    tk = vi.shape[-1]
    crow = lax.broadcasted_iota(jnp.int32, (10 * 8, tk), 0)
    ca = jnp.where(crow < 16, 0, jnp.where(crow < 72, 1 + (crow - 16) // 8, crow - 64))
    cb = jnp.where(crow < 16, crow, jnp.where(crow < 72, (crow - 16) % 8, 0))
    stair = (ca + 1) * (cb + 1) <= TOPK
    order2 = ca * TOPK + cb
    cand = jnp.concatenate([vi[0:1] + vj] + [vi[a:a + 1] + vj[0:8] for a in range(1, 8)] + [vi[8:16] + vj[0:1]],
                           axis=0)
    eid = jnp.concatenate([ii[0:1] * NKEYS + ij] + [ii[a:a + 1] * NKEYS + ij[0:8] for a in range(1, 8)]
                          + [ii[8:16] * NKEYS + ij[0:1]], axis=0)
    return jnp.where(stair, cand, -jnp.inf), eid, order2


def _second_stage(vi, ii, vj, ij):
    cand, eid, order2 = _candidates(vi, ii, vj, ij)
    vals, _, hits = _top16(cand, order2)
    sc = jnp.concatenate(vals, axis=0)
    ex = jnp.concatenate([jnp.max(jnp.where(s, eid, -1), axis=0, keepdims=True) for s in hits], axis=0)
    e = jnp.exp(sc - sc[0:1])
    return ex, e / jnp.sum(e, axis=0, keepdims=True)


def _topk_head(s_i, s_j):
    rowi = lax.broadcasted_iota(jnp.int32, s_i.shape, 0)
    tops = []
    for s in (s_i, s_j):
        vals, picks, _ = _top16(s, rowi)
        tops.append((jnp.concatenate(vals, axis=0), jnp.concatenate(picks, axis=0)))
    (vi, ii), (vj, ij) = tops
    return _second_stage(vi, ii, vj, ij)


def _gelu(x):
    return 0.5 * x * (1.0 + lax.erf(x * (1.0 / math.sqrt(2.0))))


NSLOT = 4
TBLK = 2 * 8


def _peer_kernel(st0_ref, stn_ref, hx_ref, x1_ref, gt2_ref, g3_ref, uv_hbm, o_ref, *scratch):
    bufs = scratch[:NSLOT]
    sem, ids_sm, ids_v, g_s, sem_ids, idst, gt, vals_s, idx_s, acc = scratch[NSLOT:]
    tk, d = hx_ref.shape
    nc = d // 128
    step = pl.program_id(0)
    cur = step % 2
    nxt = 1 - cur

    def issue(half, t, slot):
        for p in range(NPAIR):
            e = ids_sm[half * tk + t, p]
            pltpu.make_async_copy(uv_hbm.at[e], bufs[slot].at[:, p, :], sem.at[slot]).start(priority=p % 2)

    def wait(slot):
        pltpu.make_async_copy(bufs[slot], bufs[slot], sem.at[slot]).wait()

    def publish(half):
        g_s[half] = gt[...].T
        ids_v[...] = idst[...].T
        cp = pltpu.make_async_copy(ids_v, ids_sm.at[pl.ds(half * tk, tk)], sem_ids)
        cp.start()
        cp.wait()

    def retrieve(st_ref, h):
        ex, g = _topk_head(st_ref[2 * h], st_ref[2 * h + 1])
        idst[pl.ds(h * TOPK, TOPK), :] = ex
        gt[pl.ds(h * TOPK, TOPK), :] = g

    @pl.when(step == 0)
    def _():
        for h in range(PEER_HEADS):
            retrieve(st0_ref, h)
        publish(0)
        vals_s[...] = jnp.zeros_like(vals_s)
        idx_s[...] = jnp.zeros_like(idx_s)
        for t in range(NSLOT):
            issue(0, t, t)

    rowi = lax.broadcasted_iota(jnp.int32, (8, NPAIR), 0)

    rowk = lax.broadcasted_iota(jnp.int32, (NKEYS, tk), 0)
    nblk = tk // TBLK

    def expert_token(slot, j, x8, g8, o8):
        wait(slot)
        u = jnp.concatenate([bufs[slot][s] for s in range(nc)], axis=1).astype(BF16)
        vv = jnp.concatenate([bufs[slot][nc + s] for s in range(nc)], axis=1).astype(BF16)
        act = lax.dot_general(x8, u, (((1,), (1,)), ((), ())), preferred_element_type=F32)
        w = jnp.where(rowi == j, g8 * _gelu(act), 0.0)
        dep = ((lax.bitcast_convert_type(act[0:1], jnp.uint32) >> 16) >> 16).astype(jnp.int32)
        return o8 + jnp.dot(w.astype(BF16), vv, preferred_element_type=F32), dep

    def first_stage_pick(state, out, dep):
        mi, ai, _, s_i = _sel_step(state[0], rowk, dep)
        mj, aj, _, s_j = _sel_step(state[1], rowk, dep)
        for k, val in (("vi", mi), ("ii", ai), ("vj", mj), ("ij", aj)):
            out[k].append(val)
        return s_i, s_j

    def second_stage_pick(cand, eid, order2, out, dep):
        mc, _, hit, cand = _sel_step(cand, order2, dep)
        out["sc"].append(mc)
        out["ex"].append(jnp.max(jnp.where(hit, eid, -1), axis=0, keepdims=True))
        return cand

    def store_head(h, out):
        sc = jnp.concatenate(out["sc"], axis=0)
        e = jnp.exp(sc - sc[0:1])
        idst[pl.ds(h * TOPK, TOPK), :] = jnp.concatenate(out["ex"], axis=0)
        gt[pl.ds(h * TOPK, TOPK), :] = e / jnp.sum(e, axis=0, keepdims=True)

    def block(m):
        lists = (stn_ref[2 * m], stn_ref[2 * m + 1])
        cand, eid, order2 = _candidates(vals_s[0], idx_s[0], vals_s[1], idx_s[1])
        first = {k: [] for k in ("vi", "ii", "vj", "ij")}
        second = {"sc": [], "ex": []}
        for half8 in range(TBLK // 8):
            base = pl.multiple_of(m * TBLK + half8 * 8, 8)
            x8 = hx_ref[pl.ds(base, 8), :].astype(BF16)
            g8 = g_s[cur, pl.ds(base, 8), :]
            o8 = jnp.zeros((8, d), F32)
            for j in range(8):
                slot = j % NSLOT
                o8, dep = expert_token(slot, j, x8, g8, o8)
                issue(cur, base + j + NSLOT, slot)
                lists = first_stage_pick(lists, first, dep)
                cand = second_stage_pick(cand, eid, order2, second, dep)
            acc[pl.ds(base, 8), :] = o8
        store_head((m + PEER_HEADS - 1) % PEER_HEADS, second)
        vals_s[0] = jnp.concatenate(first["vi"], axis=0)
        idx_s[0] = jnp.concatenate(first["ii"], axis=0)
        vals_s[1] = jnp.concatenate(first["vj"], axis=0)
        idx_s[1] = jnp.concatenate(first["ij"], axis=0)

    def last_block():
        m = nblk - 1
        quick = (3, 3, 3, 3, 2, 2)
        lists = (stn_ref[2 * m], stn_ref[2 * m + 1])
        cand_p, eid_p, order2 = _candidates(vals_s[0], idx_s[0], vals_s[1], idx_s[1])
        first = {k: [] for k in ("vi", "ii", "vj", "ij")}
        second_p = {"sc": [], "ex": []}
        second_m = {"sc": [], "ex": []}
        cand_m = eid_m = None
        for q in range(TBLK):
            half8, j = divmod(q, 8)
            base = m * TBLK + half8 * 8
            if j == 0:
                x8 = hx_ref[pl.ds(base, 8), :].astype(BF16)
                g8 = g_s[cur, pl.ds(base, 8), :]
                o8 = jnp.zeros((8, d), F32)
            slot = j % NSLOT
            o8, dep = expert_token(slot, j, x8, g8, o8)
            if q + NSLOT < TBLK:
                issue(cur, base + j + NSLOT, slot)
            else:
                issue(nxt, q + NSLOT - TBLK, slot)
            if q < len(quick):
                for _ in range(quick[q]):
                    lists = first_stage_pick(lists, first, dep)
                    cand_p = second_stage_pick(cand_p, eid_p, order2, second_p, dep)
            elif q < 2 * len(quick):
                if q == len(quick):
                    cand_m, eid_m, _ = _candidates(*(jnp.concatenate(first[k], axis=0) for k in ("vi", "ii", "vj", "ij")))
                for _ in range(quick[q - len(quick)]):
                    cand_m = second_stage_pick(cand_m, eid_m, order2, second_m, dep)
            if q == 2 * len(quick) - 1:
                store_head(m - 1, second_p)
                store_head(m, second_m)
                publish(nxt)
            if j == 7:
                acc[pl.ds(base, 8), :] = o8

    def body(m, carry):
        block(m)
        return carry

    lax.fori_loop(0, nblk - 1, body, 0)
    last_block()

    @pl.when(step + 1 == pl.num_programs(0))
    def _():
        for t in range(NSLOT):
            wait(t)

    o_ref[...] = x1_ref[...] + gt2_ref[0] * (_rms(acc[...]) * g3_ref[...])


def _peer_call(st, hx, x1, gt2, g3, uv, tokens_per_batch):
    n, d = hx.shape
    nl, nk, _ = st.shape
    assert nl == 2 * PEER_HEADS and nk == NKEYS and TK // TBLK == PEER_HEADS and NPAIR == 128
    nsteps = n // TK
    per_b = tokens_per_batch // TK
    tok_spec = pl.BlockSpec((TK, d), lambda i: (i, 0))
    return pl.pallas_call(
        _peer_kernel,
        grid=(nsteps,),
        in_specs=[pl.BlockSpec((nl, nk, TK), lambda i: (0, 0, 0)),
                  pl.BlockSpec((nl, nk, TK), lambda i: (0, 0, jnp.minimum(i + 1, nsteps - 1))),
                  tok_spec, tok_spec,
                  pl.BlockSpec((1, 1, d), lambda i: (i // per_b, 0, 0)),
                  _const_spec(g3.shape),
                  pl.BlockSpec(memory_space=pl.ANY)],
        out_specs=tok_spec,
        out_shape=jax.ShapeDtypeStruct((n, d), F32),
        scratch_shapes=[pltpu.VMEM((2 * d // 128, NPAIR, 128), F32) for _ in range(NSLOT)]
        + [pltpu.SemaphoreType.DMA((NSLOT,)),
           pltpu.SMEM((2 * TK, NPAIR), jnp.int32),
           pltpu.VMEM((TK, NPAIR), jnp.int32),
           pltpu.VMEM((2, TK, NPAIR), F32),
           pltpu.SemaphoreType.DMA(()),
           pltpu.VMEM((NPAIR, TK), jnp.int32),
           pltpu.VMEM((NPAIR, TK), F32),
           pltpu.VMEM((2, TOPK, TK), F32),
           pltpu.VMEM((2, TOPK, TK), jnp.int32),
           pltpu.VMEM((TK, d), F32)],
        compiler_params=_cparams(("arbitrary",)),
        name="peer",
    )(st, st, hx, x1, gt2, g3, uv)


def kernel(x, c, ctx, c_ctx, w_mod, b_mod, norm_gains, w_in, w_out, conv_w, tshift_mu, decay_w0, decay_w2, iclr_a0,
           iclr_a2, gate_w2, k_k, k_a, r_k, lnx_g, lnx_b, peer_wq, peer_subkeys, peer_u, peer_v):
    b, t, d = x.shape
    n_ctx = ctx.shape[1]
    assert w_mod.shape[0] == 1 and d == 2 * DR and t % TM == 0 and n_ctx == TM and b + 1 <= 8
    ng = norm_gains[0]

    cin = jnp.concatenate([c, c_ctx[None], jnp.zeros((8 - b - 1, d), F32)], axis=0)
    mod = _mod_call(cin, w_mod[0], b_mod[0][None])
    lat = mod[:b].reshape(b, 6, d)
    cm = mod[b].reshape(6, d)

    def both(j):
        return jnp.stack([jnp.broadcast_to(cm[j], (b, d)), lat[:, j]], axis=1)[:, :, None, :]

    zeros = jnp.zeros((HEAD, DR), F32)
    w2 = jnp.concatenate([jnp.concatenate([decay_w2[0, 0], zeros], axis=1),
                          jnp.concatenate([zeros, decay_w2[0, 1]], axis=1)], axis=0)
    a2 = jnp.concatenate([jnp.concatenate([iclr_a2[0, 0], zeros], axis=1),
                          jnp.concatenate([zeros, iclr_a2[0, 1]], axis=1)], axis=0)
    hidx = jnp.arange(DR) // HEAD
    gsum = (hidx[:, None] == hidx[None, :]).astype(BF16)

    feats = _prep_call(ctx, x, both(0), both(1), ng[0][None], w_in[0].astype(BF16), tshift_mu[0].reshape(6, DR),
                       k_k[0][None], k_a[0][None], r_k[0].reshape(1, DR), decay_w0[0].reshape(1, 2 * DR), w2,
                       iclr_a0[0].reshape(1, 2 * DR), a2.astype(BF16), gate_w2[0].astype(BF16), conv_w[0], gsum)
    r, kap, v, kdf, kdb, akf, akb, lwf, lwb, yconv, gate, bonus = feats

    yf, yb = _scan_call(r, kap, v, kdf, kdb, akf, akb, lwf, lwb, n_ctx)

    sk = peer_subkeys[0].reshape(2 * PEER_HEADS, NKEYS, -1).astype(BF16)
    x1, hx, st = _post_call(yf, yb, bonus, gate, yconv, x, lat[:, 2][:, None], lat[:, 3][:, None], lat[:, 4][:, None],
                            ng[1][None], ng[2][None], lnx_g[0][None], lnx_b[0][None], w_out[0].astype(BF16),
                            peer_wq[0].astype(BF16), sk, gsum, n_ctx)

    uv = jnp.concatenate([peer_u[0], peer_v[0]], axis=1).reshape(-1, 2 * d // 128, 128)
    out = _peer_call(st, hx.reshape(b * t, d), x1.reshape(b * t, d), lat[:, 5][:, None], ng[3][None], uv, t)
    return out.reshape(b, t, d)
```
